```python
import jax, jax.numpy as jnp
from jax import lax
import numpy as np

D_MODEL = 2048
BATCH = 4
SEQ = 2048
DEPTH = 4

N_MIXERS = 2
FOURIER_GROUPS = 8
FOURIER_GROUP_DIM = D_MODEL // FOURIER_GROUPS
MLA_HEADS = 16
QK_NOPE_DIM = 128
QK_ROPE_DIM = 64
V_HEAD_DIM = 128
Q_LORA_RANK = 768
KV_LORA_RANK = 512
ROPE_THETA = 10000.0
Q_BLOCK = 128
D_FF = 5632
N_EXPERTS = 8
TOP_K = 2
LN_EPS = 1e-5
RMS_EPS = 1e-6
DEEPNORM_ALPHA = (2.0 * DEPTH) ** 0.25
DEEPNORM_BETA = (8.0 * DEPTH) ** -0.25
N_EVEN_LAYERS = (DEPTH + 1) // 2
N_ODD_LAYERS = DEPTH // 2

kernel_name = "hybrid_fnet_mla_moe_deepnorm_encoder"


def layer_norm(x, g, b):
    x32 = x.astype(jnp.float32)
    mu = jnp.mean(x32, axis=-1, keepdims=True)
    xc = x32 - mu
    var = jnp.mean(jnp.square(xc), axis=-1, keepdims=True)
    y = xc * lax.rsqrt(var + LN_EPS) * g.astype(jnp.float32) + b.astype(jnp.float32)
    return y.astype(x.dtype)


def rms_norm(x, g):
    x32 = x.astype(jnp.float32)
    y = x32 * lax.rsqrt(jnp.mean(jnp.square(x32), axis=-1, keepdims=True) + RMS_EPS)
    return (y * g.astype(jnp.float32)).astype(x.dtype)


def rope_tables(positions):
    inv_freq = ROPE_THETA ** (-jnp.arange(0, QK_ROPE_DIM, 2, dtype=jnp.float32) / QK_ROPE_DIM)
    ang = positions.astype(jnp.float32)[..., None] * inv_freq
    return jnp.cos(ang), jnp.sin(ang)


def apply_rope(x, cos, sin):
    half = x.shape[-1] // 2
    x1 = x[..., :half].astype(jnp.float32)
    x2 = x[..., half:].astype(jnp.float32)
    out = jnp.concatenate([x1 * cos - x2 * sin, x2 * cos + x1 * sin], axis=-1)
    return out.astype(x.dtype)


def fourier_mixer(x, w_o):
    B, S, D = x.shape
    xg = x.astype(jnp.float32).reshape(B, S, FOURIER_GROUPS, FOURIER_GROUP_DIM)
    mixed = jnp.fft.fft2(xg, axes=(1, 3), norm="ortho").real
    return mixed.astype(x.dtype).reshape(B, S, D) @ w_o


def mla_mixer(x, cos, sin, w_in, q_norm_g, w_uq, kv_norm_g, w_ukv, w_o):
    B, S, D = x.shape
    H = MLA_HEADS
    lat = x @ w_in
    q_lat = lat[..., :Q_LORA_RANK]
    kv_lat = lat[..., Q_LORA_RANK:Q_LORA_RANK + KV_LORA_RANK]
    k_pe = lat[..., Q_LORA_RANK + KV_LORA_RANK:]
    q = (rms_norm(q_lat, q_norm_g) @ w_uq).reshape(B, S, H, QK_NOPE_DIM + QK_ROPE_DIM)
    q_nope = q[..., :QK_NOPE_DIM]
    q_pe = apply_rope(q[..., QK_NOPE_DIM:], cos[:, :, None, :], sin[:, :, None, :])
    k_pe = apply_rope(k_pe, cos, sin)
    kv = (rms_norm(kv_lat, kv_norm_g) @ w_ukv).reshape(B, S, H, QK_NOPE_DIM + V_HEAD_DIM)
    k_nope = kv[..., :QK_NOPE_DIM]
    v = kv[..., QK_NOPE_DIM:]
    scale = (QK_NOPE_DIM + QK_ROPE_DIM) ** -0.5
    n_blk = S // Q_BLOCK
    qn_b = q_nope.reshape(B, n_blk, Q_BLOCK, H, QK_NOPE_DIM).transpose(1, 0, 2, 3, 4)
    qp_b = q_pe.reshape(B, n_blk, Q_BLOCK, H, QK_ROPE_DIM).transpose(1, 0, 2, 3, 4)

    def attend(blk):
        qn, qp = blk
        s = (jnp.einsum('bqhd,bkhd->bhqk', qn, k_nope)
             + jnp.einsum('bqhr,bkr->bhqk', qp, k_pe)).astype(jnp.float32) * scale
        p = jax.nn.softmax(s, axis=-1).astype(v.dtype)
        return jnp.einsum('bhqk,bkhd->bqhd', p, v)

    o = lax.map(attend, (qn_b, qp_b))
    o = o.transpose(1, 0, 2, 3, 4).reshape(B, S, H * V_HEAD_DIM)
    return o @ w_o


def swiglu(x, w_gate, w_up, w_down):
    return (jax.nn.silu(x @ w_gate) * (x @ w_up)) @ w_down


def moe_swiglu(x, w_router, w_gate, w_up, w_down):
    B, S, D = x.shape
    t = x.reshape(B * S, D)
    logits = (t @ w_router).astype(jnp.float32)
    top_vals, top_idx = lax.top_k(logits, TOP_K)
    gates = jax.nn.softmax(top_vals, axis=-1)
    combine = jnp.sum(jax.nn.one_hot(top_idx, N_EXPERTS, dtype=jnp.float32) * gates[..., None], axis=1)
    combine = combine.astype(x.dtype)
    y = jnp.zeros_like(t)
    for e in range(N_EXPERTS):
        y = y + combine[:, e:e + 1] * swiglu(t, w_gate[e], w_up[e], w_down[e])
    return y.reshape(B, S, D)


def setup_inputs(seed: int = 0) -> dict:
    key = jax.random.key(seed)
    ks = jax.random.split(key, 24)
    f32 = jnp.float32

    def nrm(k, shape, scale):
        return jax.random.normal(k, shape, f32) * scale

    D, H, F, E = D_MODEL, MLA_HEADS, D_FF, N_EXPERTS
    NE, NO = N_EVEN_LAYERS, N_ODD_LAYERS
    beta = DEEPNORM_BETA
    offset = jax.random.randint(ks[1], (BATCH, 1), 0, 4096, dtype=jnp.int32)
    positions = (jnp.arange(SEQ, dtype=jnp.int32)[None, :] + offset).astype(jnp.int32)
    return {
        "x": nrm(ks[0], (BATCH, SEQ, D), 1.0),
        "positions": positions,
        "emb_ln_g": 1.0 + nrm(ks[2], (D,), 0.02),
        "emb_ln_b": nrm(ks[3], (D,), 0.02),
        "ln_g": 1.0 + nrm(ks[4], (DEPTH, 2, D), 0.02),
        "ln_b": nrm(ks[5], (DEPTH, 2, D), 0.02),
        "fourier_w_o": nrm(ks[6], (NE, D, D), beta * D ** -0.5),
        "mla_w_in": nrm(ks[7], (NO, D, Q_LORA_RANK + KV_LORA_RANK + QK_ROPE_DIM), D ** -0.5),
        "mla_q_norm": 1.0 + nrm(ks[8], (NO, Q_LORA_RANK), 0.02),
        "mla_w_uq": nrm(ks[9], (NO, Q_LORA_RANK, H * (QK_NOPE_DIM + QK_ROPE_DIM)), Q_LORA_RANK ** -0.5),
        "mla_kv_norm": 1.0 + nrm(ks[10], (NO, KV_LORA_RANK), 0.02),
        "mla_w_ukv": nrm(ks[11], (NO, KV_LORA_RANK, H * (QK_NOPE_DIM + V_HEAD_DIM)), KV_LORA_RANK ** -0.5),
        "mla_w_o": nrm(ks[12], (NO, H * V_HEAD_DIM, D), beta * (H * V_HEAD_DIM) ** -0.5),
        "ffn_w_gate": nrm(ks[13], (NE, D, F), D ** -0.5),
        "ffn_w_up": nrm(ks[14], (NE, D, F), D ** -0.5),
        "ffn_w_down": nrm(ks[15], (NE, F, D), beta * F ** -0.5),
        "moe_w_router": nrm(ks[16], (NO, D, E), D ** -0.5),
        "moe_w_gate": nrm(ks[17], (NO, E, D, F), D ** -0.5),
        "moe_w_up": nrm(ks[18], (NO, E, D, F), D ** -0.5),
        "moe_w_down": nrm(ks[19], (NO, E, F, D), beta * F ** -0.5),
    }


def reference(x, positions, emb_ln_g, emb_ln_b, ln_g, ln_b, fourier_w_o,
              mla_w_in, mla_q_norm, mla_w_uq, mla_kv_norm, mla_w_ukv, mla_w_o,
              ffn_w_gate, ffn_w_up, ffn_w_down,
              moe_w_router, moe_w_gate, moe_w_up, moe_w_down):
    cos, sin = rope_tables(positions)
    x = layer_norm(x, emb_ln_g, emb_ln_b)
    for i in range(DEPTH):
        j = i // 2
        if i % N_MIXERS == 0:
            h = fourier_mixer(x, fourier_w_o[j])
        else:
            h = mla_mixer(x, cos, sin, mla_w_in[j], mla_q_norm[j], mla_w_uq[j],
                          mla_kv_norm[j], mla_w_ukv[j], mla_w_o[j])
        x = layer_norm(DEEPNORM_ALPHA * x + h, ln_g[i, 0], ln_b[i, 0])
        if i % 2 == 0:
            f = swiglu(x, ffn_w_gate[j], ffn_w_up[j], ffn_w_down[j])
        else:
            f = moe_swiglu(x, moe_w_router[j], moe_w_gate[j], moe_w_up[j], moe_w_down[j])
        x = layer_norm(DEEPNORM_ALPHA * x + f, ln_g[i, 1], ln_b[i, 1])
    return x
```

```python
import functools

import numpy as np
import jax
import jax.numpy as jnp
from jax import lax
from jax.experimental import pallas as pl
from jax.experimental.pallas import tpu as pltpu

F32, BF16, I32 = jnp.float32, jnp.bfloat16, jnp.int32

GROUP_DIM = 256
QK_NOPE = 128
QK_ROPE = 64
V_DIM = 128
HEAD_PAD = 256
TOP_K = 2
ROPE_THETA = 10000.0
LN_EPS = 1e-5
RMS_EPS = 1e-6
ATTN_SCALE = (QK_NOPE + QK_ROPE) ** -0.5

LANES = 128
MIB = 1024 * 1024


def _params(semantics, vmem_mib):
    return pltpu.CompilerParams(dimension_semantics=semantics, vmem_limit_bytes=vmem_mib * MIB)


def _tile(n, pref):
    t = min(n, pref)
    while n % t:
        t //= 2
    return t


def _ln_rows(v, g, b):
    mu = jnp.mean(v, axis=-1, keepdims=True)
    vc = v - mu
    var = jnp.mean(vc * vc, axis=-1, keepdims=True)
    return vc * lax.rsqrt(var + LN_EPS) * g + b


def _ln_kernel(x_ref, g_ref, b_ref, o32_ref, o16_ref):
    y = _ln_rows(x_ref[...], g_ref[...], b_ref[...])
    o32_ref[...] = y
    o16_ref[...] = y.astype(BF16)


def _add_ln_kernel(x_ref, h_ref, g_ref, b_ref, o32_ref, o16_ref, *, alpha):
    y = _ln_rows(alpha * x_ref[...] + h_ref[...], g_ref[...], b_ref[...])
    o32_ref[...] = y
    o16_ref[...] = y.astype(BF16)


def _layer_norm(x, g, b, h=None, alpha=1.0):
    n, d = x.shape
    tm = _tile(n, 512)
    row = pl.BlockSpec((tm, d), lambda i: (i, 0))
    vec = pl.BlockSpec((1, d), lambda i: (0, 0))
    g2, b2 = g.reshape(1, d), b.reshape(1, d)
    out_shape = (jax.ShapeDtypeStruct((n, d), F32), jax.ShapeDtypeStruct((n, d), BF16))
    if h is None:
        return pl.pallas_call(
            _ln_kernel, grid=(n // tm,), in_specs=[row, vec, vec], out_specs=(row, row),
            out_shape=out_shape, compiler_params=_params(("parallel",), 40), name="ln",
        )(x, g2, b2)
    return pl.pallas_call(
        functools.partial(_add_ln_kernel, alpha=alpha), grid=(n // tm,),
        in_specs=[row, row, vec, vec], out_specs=(row, row),
        out_shape=out_shape, compiler_params=_params(("parallel",), 40), name="add_ln",
    )(x, h, g2, b2)


def _mm_kernel(a_ref, w_ref, o_ref):
    o_ref[...] = jnp.dot(a_ref[...], w_ref[...].astype(BF16),
                         preferred_element_type=F32).astype(o_ref.dtype)


def _matmul(a, w, out_dtype, tm_pref=1024, tn_pref=512, name="mm"):
    m, k = a.shape
    n = w.shape[1]
    tm, tn = _tile(m, tm_pref), _tile(n, tn_pref)
    return pl.pallas_call(
        _mm_kernel, grid=(m // tm, n // tn),
        in_specs=[pl.BlockSpec((tm, k), lambda i, j: (i, 0)),
                  pl.BlockSpec((k, tn), lambda i, j: (0, j))],
        out_specs=pl.BlockSpec((tm, tn), lambda i, j: (i, j)),
        out_shape=jax.ShapeDtypeStruct((m, n), out_dtype),
        compiler_params=_params(("parallel", "parallel"), 48), name=name,
    )(a, w)


def _dft_tables(n, scale):
    k = jnp.arange(n, dtype=I32)
    ang = ((k[:, None] * k[None, :]) % n).astype(F32) * np.float32(2.0 * np.pi / n)
    return jnp.cos(ang) * np.float32(scale), jnp.sin(ang) * np.float32(scale)


def _fourier_tables(seq):
    cg, sg = _dft_tables(GROUP_DIM, GROUP_DIM ** -0.5)
    cs, ss = _dft_tables(seq, seq ** -0.5)
    return (jnp.concatenate([cg, sg], axis=1).astype(BF16),
            jnp.concatenate([cs, -ss], axis=1).astype(BF16))


def _group_dft_kernel(x_ref, t_ref, z_ref, *, groups):
    t = t_ref[...]
    for g in range(groups):
        cols = slice(g * GROUP_DIM, (g + 1) * GROUP_DIM)
        z = jnp.dot(x_ref[:, cols], t, preferred_element_type=F32)
        z_ref[0, :, cols] = z[:, :GROUP_DIM].astype(BF16)
        z_ref[1, :, cols] = z[:, GROUP_DIM:].astype(BF16)


def _seq_dft_kernel(t_ref, z_ref, y_ref):
    y_ref[...] = jnp.dot(t_ref[...], z_ref[...], preferred_element_type=F32).astype(BF16)


def _fourier_mixer(x16, w_o, tabs, batch, seq):
    n, d = x16.shape
    t_group, t_seq = tabs
    tm = _tile(seq, 512)
    z = pl.pallas_call(
        functools.partial(_group_dft_kernel, groups=d // GROUP_DIM),
        grid=(batch, seq // tm),
        in_specs=[pl.BlockSpec((None, tm, d), lambda b, i: (b, i, 0)),
                  pl.BlockSpec((GROUP_DIM, 2 * GROUP_DIM), lambda b, i: (0, 0))],
        out_specs=pl.BlockSpec((None, 2, tm, d), lambda b, i: (b, 0, i, 0)),
        out_shape=jax.ShapeDtypeStruct((batch, 2, seq, d), BF16),
        compiler_params=_params(("parallel", "parallel"), 40), name="group_dft",
    )(x16.reshape(batch, seq, d), t_group)
    tr, tn = _tile(seq, 512), _tile(d, 512)
    y = pl.pallas_call(
        _seq_dft_kernel, grid=(seq // tr, batch, d // tn),
        in_specs=[pl.BlockSpec((tr, 2 * seq), lambda r, b, j: (r, 0)),
                  pl.BlockSpec((None, 2 * seq, tn), lambda r, b, j: (b, 0, j))],
        out_specs=pl.BlockSpec((None, tr, tn), lambda r, b, j: (b, r, j)),
        out_shape=jax.ShapeDtypeStruct((batch, seq, d), BF16),
        compiler_params=_params(("parallel", "parallel", "parallel"), 40), name="seq_dft",
    )(t_seq, z.reshape(batch, 2 * seq, d))
    return _matmul(y.reshape(n, d), w_o, F32, name="fourier_out")


def _rope_tables(positions):
    inv_freq = ROPE_THETA ** (-jnp.arange(0, QK_ROPE, 2, dtype=F32) / QK_ROPE)
    ang = positions.astype(F32).reshape(-1, 1) * inv_freq
    cos, sin = jnp.cos(ang), jnp.sin(ang)
    zero = jnp.zeros((ang.shape[0], LANES - QK_ROPE), F32)
    return (jnp.concatenate([cos, cos, zero], axis=1), jnp.concatenate([-sin, sin, zero], axis=1))


def _rope_lanes(v, cos_t, sin_t):
    half = QK_ROPE // 2
    lane = lax.broadcasted_iota(I32, v.shape, 1)
    swapped = jnp.where(lane < half, pltpu.roll(v, LANES - half, 1), pltpu.roll(v, half, 1))
    return v * cos_t + swapped * sin_t


def _rms_rows(v, g):
    return v * lax.rsqrt(jnp.mean(v * v, axis=-1, keepdims=True) + RMS_EPS) * g


def _latent_kernel(x_ref, w_ref, qg_ref, kvg_ref, cos_ref, sin_ref, q_ref, kv_ref, kpe_ref,
                   *, q_rank, kv_rank):
    lat = jnp.dot(x_ref[...], w_ref[...].astype(BF16), preferred_element_type=F32)
    q_ref[...] = _rms_rows(lat[:, :q_rank], qg_ref[...]).astype(BF16)
    kv_ref[...] = _rms_rows(lat[:, q_rank:q_rank + kv_rank], kvg_ref[...]).astype(BF16)
    kpe_ref[...] = _rope_lanes(lat[:, q_rank + kv_rank:], cos_ref[...], sin_ref[...]).astype(BF16)


def _q_up_kernel(a_ref, w_ref, cos_ref, sin_ref, o_ref, *, heads):
    acc = jnp.dot(a_ref[...], w_ref[...].astype(BF16), preferred_element_type=F32)
    cos_t, sin_t = cos_ref[...], sin_ref[...]
    for h in range(heads):
        lo = h * HEAD_PAD
        o_ref[:, lo:lo + QK_NOPE] = (acc[:, lo:lo + QK_NOPE] * ATTN_SCALE).astype(BF16)
        rope = _rope_lanes(acc[:, lo + QK_NOPE:lo + HEAD_PAD], cos_t, sin_t)
        o_ref[:, lo + QK_NOPE:lo + HEAD_PAD] = (rope * ATTN_SCALE).astype(BF16)


def _attention_kernel(q_ref, kn_ref, kpe_ref, v_ref, o_ref):
    k = jnp.concatenate([kn_ref[...], kpe_ref[...]], axis=1)
    s = lax.dot_general(q_ref[...], k, (((1,), (1,)), ((), ())), preferred_element_type=F32)
    m = jnp.max(s, axis=-1, keepdims=True)
    p = jnp.exp(s - m)
    l = jnp.sum(p, axis=-1, keepdims=True)
    o = jnp.dot(p.astype(BF16), v_ref[...], preferred_element_type=F32)
    o_ref[...] = (o / l).astype(BF16)


def _mla_mixer(x16, cos_t, sin_t, w_in, q_norm, w_uq, kv_norm, w_ukv, w_o, batch, seq):
    n, d = x16.shape
    q_rank, kv_rank = q_norm.shape[0], kv_norm.shape[0]
    heads = w_uq.shape[1] // (QK_NOPE + QK_ROPE)
    w_in_p = jnp.pad(w_in, ((0, 0), (0, LANES - QK_ROPE)))
    w_uq_p = jnp.pad(w_uq.reshape(q_rank, heads, QK_NOPE + QK_ROPE),
                     ((0, 0), (0, 0), (0, HEAD_PAD - QK_NOPE - QK_ROPE))).reshape(q_rank, heads * HEAD_PAD)
    w_ukv_p = w_ukv.reshape(kv_rank, heads, 2, QK_NOPE).transpose(0, 2, 1, 3).reshape(kv_rank, 2 * heads * QK_NOPE)
    lat_w = q_rank + kv_rank + LANES

    tm = _tile(n, 512)
    row = lambda w: pl.BlockSpec((tm, w), lambda i: (i, 0))
    q_lat, kv_lat, k_pe = pl.pallas_call(
        functools.partial(_latent_kernel, q_rank=q_rank, kv_rank=kv_rank), grid=(n // tm,),
        in_specs=[row(d), pl.BlockSpec((d, lat_w), lambda i: (0, 0)),
                  pl.BlockSpec((1, q_rank), lambda i: (0, 0)), pl.BlockSpec((1, kv_rank), lambda i: (0, 0)),
                  row(LANES), row(LANES)],
        out_specs=(row(q_rank), row(kv_rank), row(LANES)),
        out_shape=(jax.ShapeDtypeStruct((n, q_rank), BF16), jax.ShapeDtypeStruct((n, kv_rank), BF16),
                   jax.ShapeDtypeStruct((n, LANES), BF16)),
        compiler_params=_params(("parallel",), 48), name="mla_latent",
    )(x16, w_in_p, q_norm.reshape(1, -1), kv_norm.reshape(1, -1), cos_t, sin_t)

    tq_m, hb = _tile(n, 1024), _tile(heads, 4)
    q_cat = pl.pallas_call(
        functools.partial(_q_up_kernel, heads=hb), grid=(n // tq_m, heads // hb),
        in_specs=[pl.BlockSpec((tq_m, q_rank), lambda i, j: (i, 0)),
                  pl.BlockSpec((q_rank, hb * HEAD_PAD), lambda i, j: (0, j)),
                  pl.BlockSpec((tq_m, LANES), lambda i, j: (i, 0)),
                  pl.BlockSpec((tq_m, LANES), lambda i, j: (i, 0))],
        out_specs=pl.BlockSpec((tq_m, hb * HEAD_PAD), lambda i, j: (i, j)),
        out_shape=jax.ShapeDtypeStruct((n, heads * HEAD_PAD), BF16),
        compiler_params=_params(("parallel", "parallel"), 40), name="mla_q_up",
    )(q_lat, w_uq_p, cos_t, sin_t)

    kv = _matmul(kv_lat, w_ukv_p, BF16, name="mla_kv_up")

    tq = _tile(seq, 512)
    o = pl.pallas_call(
        _attention_kernel, grid=(batch, heads, seq // tq),
        in_specs=[pl.BlockSpec((None, tq, HEAD_PAD), lambda b, h, i: (b, i, h)),
                  pl.BlockSpec((None, seq, QK_NOPE), lambda b, h, i: (b, 0, h)),
                  pl.BlockSpec((None, seq, LANES), lambda b, h, i: (b, 0, 0)),
                  pl.BlockSpec((None, seq, V_DIM), lambda b, h, i: (b, 0, heads + h))],
        out_specs=pl.BlockSpec((None, tq, V_DIM), lambda b, h, i: (b, i, h)),
        out_shape=jax.ShapeDtypeStruct((batch, seq, heads * V_DIM), BF16),
        compiler_params=_params(("parallel", "parallel", "parallel"), 40), name="mla_attention",
    )(q_cat.reshape(batch, seq, -1), kv.reshape(batch, seq, -1), k_pe.reshape(batch, seq, -1),
      kv.reshape(batch, seq, -1))
    return _matmul(o.reshape(n, -1), w_o, F32, name="mla_out")


def _ffn_kernel(te_ref, nt_ref, x_ref, wg_ref, wu_ref, wd_ref, *rest, has_gate):
    if has_gate:
        gs_ref, o_ref, acc_ref = rest
    else:
        o_ref, acc_ref = rest
    i, j = pl.program_id(0), pl.program_id(1)

    @pl.when(j == 0)
    def _():
        acc_ref[...] = jnp.zeros_like(acc_ref)

    @pl.when(i < nt_ref[0])
    def _():
        x = x_ref[...].astype(BF16)
        g = jnp.dot(x, wg_ref[...].astype(BF16), preferred_element_type=F32)
        u = jnp.dot(x, wu_ref[...].astype(BF16), preferred_element_type=F32)
        h = (g / (1.0 + jnp.exp(-g)) * u).astype(BF16)
        acc_ref[...] += jnp.dot(h, wd_ref[...].astype(BF16), preferred_element_type=F32)

    @pl.when(j == pl.num_programs(1) - 1)
    def _():
        o_ref[...] = acc_ref[...] * gs_ref[...] if has_gate else acc_ref[...]


def _ffn(x, w_gate, w_up, w_down, tile_expert, n_tiles, row_gate, tm):
    r, d = x.shape
    f = w_gate.shape[-1]
    tf = _tile(f, 512)
    nf = f // tf

    def w_col(i, j, te, nt):
        return (te[i], 0, jnp.where(i < nt[0], j, nf - 1))

    def w_row(i, j, te, nt):
        return (te[i], jnp.where(i < nt[0], j, nf - 1), 0)

    in_specs = [pl.BlockSpec((tm, d), lambda i, j, te, nt: (i, 0)),
                pl.BlockSpec((None, d, tf), w_col), pl.BlockSpec((None, d, tf), w_col),
                pl.BlockSpec((None, tf, d), w_row)]
    args = [x, w_gate, w_up, w_down]
    if row_gate is not None:
        in_specs.append(pl.BlockSpec((tm, 1), lambda i, j, te, nt: (i, 0)))
        args.append(row_gate)
    return pl.pallas_call(
        functools.partial(_ffn_kernel, has_gate=row_gate is not None),
        grid_spec=pltpu.PrefetchScalarGridSpec(
            num_scalar_prefetch=2, grid=(r // tm, nf), in_specs=in_specs,
            out_specs=pl.BlockSpec((tm, d), lambda i, j, te, nt: (i, 0)),
            scratch_shapes=[pltpu.VMEM((tm, d), F32)]),
        out_shape=jax.ShapeDtypeStruct((r, d), F32),
        compiler_params=_params(("arbitrary", "arbitrary"), 56), name="ffn",
    )(tile_expert, n_tiles, *args)


def _dense_ffn(x16, w_gate, w_up, w_down):
    n = x16.shape[0]
    tm = _tile(n, 512)
    return _ffn(x16, w_gate[None], w_up[None], w_down[None],
                jnp.zeros((n // tm,), I32), jnp.full((1,), n // tm, I32), None, tm)


def _router_kernel(x_ref, w_ref, idx_ref, gate_ref, rank_ref, cnt_ref, carry_ref):
    tm = x_ref.shape[0]
    n_exp = w_ref.shape[0]

    @pl.when(pl.program_id(0) == 0)
    def _():
        carry_ref[...] = jnp.zeros_like(carry_ref)

    x = x_ref[...]
    xh = x.astype(BF16)
    xl = (x - xh.astype(F32)).astype(BF16)
    w = w_ref[...]
    wh = w.astype(BF16)
    wl = (w - wh.astype(F32)).astype(BF16)
    nt_dims = (((1,), (1,)), ((), ()))
    logits = (lax.dot_general(wh, xh, nt_dims, preferred_element_type=F32)
              + lax.dot_general(wh, xl, nt_dims, preferred_element_type=F32)
              + lax.dot_general(wl, xh, nt_dims, preferred_element_type=F32))

    e_id = lax.broadcasted_iota(I32, (n_exp, tm), 0).astype(F32)
    m1 = jnp.max(logits, axis=0, keepdims=True)
    i1 = jnp.min(jnp.where(logits == m1, e_id, float(n_exp)), axis=0, keepdims=True)
    first = e_id == i1
    rest = jnp.where(first, -jnp.inf, logits)
    m2 = jnp.max(rest, axis=0, keepdims=True)
    i2 = jnp.min(jnp.where(rest == m2, e_id, float(n_exp)), axis=0, keepdims=True)
    second = e_id == i2
    d = jnp.exp(m2 - m1)
    g1 = 1.0 / (1.0 + d)

    chosen = jnp.where(first | second, 1.0, 0.0)
    before = (lax.broadcasted_iota(I32, (tm, tm), 0) < lax.broadcasted_iota(I32, (tm, tm), 1))
    rank = jnp.dot(chosen.astype(BF16), jnp.where(before, 1.0, 0.0).astype(BF16),
                   preferred_element_type=F32) + carry_ref[:, :1]
    r1 = jnp.sum(jnp.where(first, rank, 0.0), axis=0, keepdims=True)
    r2 = jnp.sum(jnp.where(second, rank, 0.0), axis=0, keepdims=True)
    carry_ref[...] = carry_ref[...] + jnp.sum(chosen, axis=1, keepdims=True)

    idx_ref[0:1, :] = i1.astype(I32)
    idx_ref[1:2, :] = i2.astype(I32)
    gate_ref[0:1, :] = g1
    gate_ref[1:2, :] = d * g1
    rank_ref[0:1, :] = r1.astype(I32)
    rank_ref[1:2, :] = r2.astype(I32)
    cnt_ref[...] = carry_ref[...].astype(I32)


def _route(x32, w_router):
    n, d = x32.shape
    n_exp = w_router.shape[1]
    tm = _tile(n, 1024)
    slot = pl.BlockSpec((TOP_K, tm), lambda i: (0, i))
    return pl.pallas_call(
        _router_kernel, grid=(n // tm,),
        in_specs=[pl.BlockSpec((tm, d), lambda i: (i, 0)), pl.BlockSpec((n_exp, d), lambda i: (0, 0))],
        out_specs=(slot, slot, slot, pl.BlockSpec((n_exp, LANES), lambda i: (0, 0))),
        out_shape=(jax.ShapeDtypeStruct((TOP_K, n), I32), jax.ShapeDtypeStruct((TOP_K, n), F32),
                   jax.ShapeDtypeStruct((TOP_K, n), I32), jax.ShapeDtypeStruct((n_exp, LANES), I32)),
        scratch_shapes=[pltpu.VMEM((n_exp, LANES), F32)],
        compiler_params=_params(("arbitrary",), 40), name="router",
    )(x32, w_router.T)


def _gather_rows_kernel(src_ref, x_hbm, o_ref, sem):
    rows = o_ref.shape[0]
    base = pl.program_id(0) * rows

    def row_copy(r, src_row):
        return pltpu.make_async_copy(x_hbm.at[src_row], o_ref.at[r], sem)

    def issue(r, c):
        row_copy(r, src_ref[base + r]).start()
        return c

    def drain(r, c):
        row_copy(r, 0).wait()
        return c

    lax.fori_loop(0, rows, issue, 0)
    lax.fori_loop(0, rows, drain, 0)


def _gather_rows(src, x, n_out):
    d = x.shape[1]
    tg = _tile(n_out, 256)
    return pl.pallas_call(
        _gather_rows_kernel,
        grid_spec=pltpu.PrefetchScalarGridSpec(
            num_scalar_prefetch=1, grid=(n_out // tg,),
            in_specs=[pl.BlockSpec(memory_space=pl.ANY)],
            out_specs=pl.BlockSpec((tg, d), lambda i, s: (i, 0)),
            scratch_shapes=[pltpu.SemaphoreType.DMA(())]),
        out_shape=jax.ShapeDtypeStruct((n_out, d), x.dtype),
        compiler_params=_params(("arbitrary",), 40), name="moe_dispatch",
    )(src, x)


def _combine_kernel(pos_ref, x_ref, g_ref, b_ref, y_hbm, o32_ref, o16_ref, buf, sem, *, alpha, n_tok):
    rows = x_ref.shape[0]
    base = pl.program_id(0) * rows

    def row_copy(k, r, src_row):
        return pltpu.make_async_copy(y_hbm.at[src_row], buf.at[k, r], sem)

    def issue(r, c):
        for k in range(TOP_K):
            row_copy(k, r, pos_ref[k * n_tok + base + r]).start()
        return c

    def drain(r, c):
        for k in range(TOP_K):
            row_copy(k, r, 0).wait()
        return c

    lax.fori_loop(0, rows, issue, 0)
    lax.fori_loop(0, rows, drain, 0)
    y = _ln_rows(alpha * x_ref[...] + (buf[0] + buf[1]), g_ref[...], b_ref[...])
    o32_ref[...] = y
    o16_ref[...] = y.astype(BF16)


def _combine(pos, x32, y_sorted, g, b, alpha):
    n, d = x32.shape
    tc = _tile(n, 256)
    row = pl.BlockSpec((tc, d), lambda i, p: (i, 0))
    vec = pl.BlockSpec((1, d), lambda i, p: (0, 0))
    return pl.pallas_call(
        functools.partial(_combine_kernel, alpha=alpha, n_tok=n),
        grid_spec=pltpu.PrefetchScalarGridSpec(
            num_scalar_prefetch=1, grid=(n // tc,),
            in_specs=[row, vec, vec, pl.BlockSpec(memory_space=pl.ANY)],
            out_specs=(row, row),
            scratch_shapes=[pltpu.VMEM((TOP_K, tc, d), F32), pltpu.SemaphoreType.DMA(())]),
        out_shape=(jax.ShapeDtypeStruct((n, d), F32), jax.ShapeDtypeStruct((n, d), BF16)),
        compiler_params=_params(("arbitrary",), 40), name="moe_combine",
    )(pos.reshape(-1), x32, g.reshape(1, d), b.reshape(1, d), y_sorted)


def _moe_layer(x32, w_router, w_gate, w_up, w_down, ln_g, ln_b, alpha):
    n, d = x32.shape
    n_exp = w_router.shape[1]
    tm = _tile(n, 512)
    idx, gates, rank, cnt = _route(x32, w_router)

    n_tiles_max = (TOP_K * n) // tm + n_exp
    n_rows = n_tiles_max * tm
    tiles_e = (cnt[:, 0] + tm - 1) // tm
    tile_end = jnp.cumsum(tiles_e)
    row_off = (tile_end - tiles_e) * tm
    pos = rank + sum(jnp.where(idx == e, row_off[e], 0) for e in range(n_exp))
    n_tiles = tile_end[-1:]
    tile_id = jnp.arange(n_tiles_max, dtype=I32)
    tile_expert = jnp.sum((tile_id[:, None] >= tile_end[None, :]).astype(I32), axis=1)
    last_expert = jnp.sum((n_tiles - 1 >= tile_end).astype(I32))
    tile_expert = jnp.minimum(jnp.where(tile_id < n_tiles, tile_expert, last_expert), n_exp - 1)
    flat_pos = pos.reshape(-1)
    tok = jnp.tile(jnp.arange(n, dtype=I32), TOP_K)
    src = jnp.zeros((n_rows,), I32).at[flat_pos].set(tok)
    row_gate = jnp.zeros((n_rows,), F32).at[flat_pos].set(gates.reshape(-1)).reshape(n_rows, 1)

    x_sorted = _gather_rows(src, x32, n_rows)
    y_sorted = _ffn(x_sorted, w_gate, w_up, w_down, tile_expert, n_tiles, row_gate, tm)
    return _combine(pos, x32, y_sorted, ln_g, ln_b, alpha)


def kernel(x, positions, emb_ln_g, emb_ln_b, ln_g, ln_b, fourier_w_o, mla_w_in, mla_q_norm, mla_w_uq, mla_kv_norm, mla_w_ukv, mla_w_o, ffn_w_gate, ffn_w_up, ffn_w_down, moe_w_router, moe_w_gate, moe_w_up, moe_w_down):
    batch, seq, d = x.shape
    depth = ln_g.shape[0]
    alpha = (2.0 * depth) ** 0.25
    cos_t, sin_t = _rope_tables(positions)
    dft_tabs = _fourier_tables(seq)

    x32, x16 = _layer_norm(x.reshape(batch * seq, d), emb_ln_g, emb_ln_b)
    for i in range(depth):
        j = i // 2
        if i % 2 == 0:
            h = _fourier_mixer(x16, fourier_w_o[j], dft_tabs, batch, seq)
        else:
            h = _mla_mixer(x16, cos_t, sin_t, mla_w_in[j], mla_q_norm[j], mla_w_uq[j],
                           mla_kv_norm[j], mla_w_ukv[j], mla_w_o[j], batch, seq)
        x32, x16 = _layer_norm(x32, ln_g[i, 0], ln_b[i, 0], h, alpha)
        if i % 2 == 0:
            f = _dense_ffn(x16, ffn_w_gate[j], ffn_w_up[j], ffn_w_down[j])
            x32, x16 = _layer_norm(x32, ln_g[i, 1], ln_b[i, 1], f, alpha)
        else:
            x32, x16 = _moe_layer(x32, moe_w_router[j], moe_w_gate[j], moe_w_up[j], moe_w_down[j],
                                  ln_g[i, 1], ln_b[i, 1], alpha)
    return x32.reshape(batch, seq, d)
```

```python
import functools

import numpy as np
import jax
import jax.numpy as jnp
from jax import lax
from jax.experimental import pallas as pl
from jax.experimental.pallas import tpu as pltpu

F32, BF16, I32 = jnp.float32, jnp.bfloat16, jnp.int32

GROUP_DIM = 256
QK_NOPE = 128
QK_ROPE = 64
V_DIM = 128
HEAD_PAD = 256
TOP_K = 2
ROPE_THETA = 10000.0
LN_EPS = 1e-5
RMS_EPS = 1e-6
ATTN_SCALE = (QK_NOPE + QK_ROPE) ** -0.5
LOG2_E = 1.4426950408889634

LANES = 128
MIB = 1024 * 1024

MOE_TILE = 1024
MOE_SUB = 256


def _params(semantics, vmem_mib, flags=None):
    return pltpu.CompilerParams(dimension_semantics=semantics, vmem_limit_bytes=vmem_mib * MIB, flags=flags)


def _tile(n, pref):
    t = min(n, pref)
    while n % t:
        t //= 2
    return t


def _resident(block_shape, index_map):
    return pl.BlockSpec(block_shape, index_map, pipeline_mode=pl.Buffered(1))


def _ln_rows(v, g, b):
    mu = jnp.mean(v, axis=-1, keepdims=True)
    vc = v - mu
    var = jnp.mean(vc * vc, axis=-1, keepdims=True)
    return vc * lax.rsqrt(var + LN_EPS) * g + b


def _ln_kernel(x_ref, g_ref, b_ref, o32_ref, o16_ref):
    y = _ln_rows(x_ref[...], g_ref[...], b_ref[...])
    o32_ref[...] = y
    o16_ref[...] = y.astype(BF16)


def _add_ln_kernel(x_ref, h_ref, g_ref, b_ref, o32_ref, o16_ref, *, alpha):
    y = _ln_rows(alpha * x_ref[...] + h_ref[...], g_ref[...], b_ref[...])
    o32_ref[...] = y
    o16_ref[...] = y.astype(BF16)


def _layer_norm(x, g, b, h=None, alpha=1.0):
    n, d = x.shape
    tm = _tile(n, 512)
    row = pl.BlockSpec((tm, d), lambda i: (i, 0))
    vec = pl.BlockSpec((1, d), lambda i: (0, 0))
    g2, b2 = g.reshape(1, d), b.reshape(1, d)
    out_shape = (jax.ShapeDtypeStruct((n, d), F32), jax.ShapeDtypeStruct((n, d), BF16))
    if h is None:
        return pl.pallas_call(
            _ln_kernel, grid=(n // tm,), in_specs=[row, vec, vec], out_specs=(row, row),
            out_shape=out_shape, compiler_params=_params(("parallel",), 40), name="ln",
        )(x, g2, b2)
    return pl.pallas_call(
        functools.partial(_add_ln_kernel, alpha=alpha), grid=(n // tm,),
        in_specs=[row, row, vec, vec], out_specs=(row, row),
        out_shape=out_shape, compiler_params=_params(("parallel",), 40), name="add_ln",
    )(x, h, g2, b2)


def _cast_weight_once(w_ref, w16_ref):
    @pl.when(pl.program_id(0) == 0)
    def _():
        w16_ref[...] = w_ref[...].astype(BF16)


def _mm_kernel(a_ref, w_ref, o_ref, w16_ref):
    _cast_weight_once(w_ref, w16_ref)
    o_ref[...] = jnp.dot(a_ref[...], w16_ref[...], preferred_element_type=F32).astype(o_ref.dtype)


def _matmul(a, w_stack, layer, out_dtype, tm_pref=512, name="mm"):
    m, k = a.shape
    n = w_stack.shape[2]
    tm = _tile(m, tm_pref)
    return pl.pallas_call(
        _mm_kernel, grid=(m // tm,),
        in_specs=[pl.BlockSpec((tm, k), lambda i: (i, 0)),
                  _resident((None, k, n), lambda i: (layer, 0, 0))],
        out_specs=pl.BlockSpec((tm, n), lambda i: (i, 0)),
        out_shape=jax.ShapeDtypeStruct((m, n), out_dtype),
        scratch_shapes=[pltpu.VMEM((k, n), BF16)],
        compiler_params=_params(("arbitrary",), 56), name=name,
    )(a, w_stack)


def _mm_add_ln_kernel(a_ref, w_ref, x_ref, g_ref, b_ref, o32_ref, o16_ref, w16_ref, *, alpha):
    _cast_weight_once(w_ref, w16_ref)
    h = jnp.dot(a_ref[...], w16_ref[...], preferred_element_type=F32)
    y = _ln_rows(alpha * x_ref[...] + h, g_ref[...], b_ref[...])
    o32_ref[...] = y
    o16_ref[...] = y.astype(BF16)


def _matmul_add_ln(a, w_stack, layer, x32, g, b, alpha, name):
    m, k = a.shape
    d = w_stack.shape[2]
    tm = _tile(m, 256)
    row = lambda w: pl.BlockSpec((tm, w), lambda i: (i, 0))
    vec = pl.BlockSpec((1, d), lambda i: (0, 0))
    return pl.pallas_call(
        functools.partial(_mm_add_ln_kernel, alpha=alpha), grid=(m // tm,),
        in_specs=[row(k), _resident((None, k, d), lambda i: (layer, 0, 0)), row(d), vec, vec],
        out_specs=(row(d), row(d)),
        out_shape=(jax.ShapeDtypeStruct((m, d), F32), jax.ShapeDtypeStruct((m, d), BF16)),
        scratch_shapes=[pltpu.VMEM((k, d), BF16)],
        compiler_params=_params(("arbitrary",), 56), name=name,
    )(a, w_stack, x32, g.reshape(1, d), b.reshape(1, d))


def _dft_tables(n, scale):
    n0 = 1 << (int(np.log2(n)) // 2)
    s = jnp.arange(n, dtype=I32)[None, :]
    step = np.float32(2.0 * np.pi / n)
    hi = ((jnp.arange(n // n0, dtype=I32)[:, None] * n0 * s) % n).astype(F32) * step
    lo = ((jnp.arange(n0, dtype=I32)[:, None] * s) % n).astype(F32) * step
    ch, sh = jnp.cos(hi)[:, None, :], jnp.sin(hi)[:, None, :]
    cl, sl = jnp.cos(lo)[None, :, :], jnp.sin(lo)[None, :, :]
    cos = (ch * cl - sh * sl).reshape(n, n)
    sin = (sh * cl + ch * sl).reshape(n, n)
    return cos * np.float32(scale), sin * np.float32(scale)


def _fourier_tables(seq):
    cg, sg = _dft_tables(GROUP_DIM, GROUP_DIM ** -0.5)
    cs, ss = _dft_tables(seq, seq ** -0.5)
    return (jnp.concatenate([cg, sg], axis=1).astype(BF16),
            jnp.concatenate([cs, -ss], axis=1).astype(BF16))


def _group_dft_kernel(x_ref, t_ref, z_ref, *, groups):
    t = t_ref[...]
    for g in range(groups):
        cols = slice(g * GROUP_DIM, (g + 1) * GROUP_DIM)
        z = jnp.dot(x_ref[:, cols], t, preferred_element_type=F32)
        z_ref[0, :, cols] = z[:, :GROUP_DIM].astype(BF16)
        z_ref[1, :, cols] = z[:, GROUP_DIM:].astype(BF16)


def _seq_dft_kernel(t_ref, z_ref, y_ref):
    y_ref[...] = jnp.dot(t_ref[...], z_ref[...], preferred_element_type=F32).astype(BF16)


def _fourier_mixed(x16, tabs, batch, seq):
    n, d = x16.shape
    t_group, t_seq = tabs
    tm = _tile(seq, 512)
    z = pl.pallas_call(
        functools.partial(_group_dft_kernel, groups=d // GROUP_DIM),
        grid=(batch, seq // tm),
        in_specs=[pl.BlockSpec((None, tm, d), lambda b, i: (b, i, 0)),
                  pl.BlockSpec((GROUP_DIM, 2 * GROUP_DIM), lambda b, i: (0, 0))],
        out_specs=pl.BlockSpec((None, 2, tm, d), lambda b, i: (b, 0, i, 0)),
        out_shape=jax.ShapeDtypeStruct((batch, 2, seq, d), BF16),
        compiler_params=_params(("parallel", "parallel"), 40), name="group_dft",
    )(x16.reshape(batch, seq, d), t_group)
    tr, tn = _tile(seq, 512), _tile(d, 512)
    y = pl.pallas_call(
        _seq_dft_kernel, grid=(seq // tr, batch, d // tn),
        in_specs=[pl.BlockSpec((tr, 2 * seq), lambda r, b, j: (r, 0)),
                  pl.BlockSpec((None, 2 * seq, tn), lambda r, b, j: (b, 0, j))],
        out_specs=pl.BlockSpec((None, tr, tn), lambda r, b, j: (b, r, j)),
        out_shape=jax.ShapeDtypeStruct((batch, seq, d), BF16),
        compiler_params=_params(("parallel", "parallel", "parallel"), 40), name="seq_dft",
    )(t_seq, z.reshape(batch, 2 * seq, d))
    return y.reshape(n, d)


def _rope_tables(positions):
    inv_freq = ROPE_THETA ** (-jnp.arange(0, QK_ROPE, 2, dtype=F32) / QK_ROPE)
    ang = positions.astype(F32).reshape(-1, 1) * inv_freq
    cos, sin = jnp.cos(ang), jnp.sin(ang)
    zero = jnp.zeros((ang.shape[0], LANES - QK_ROPE), F32)
    return (jnp.concatenate([cos, cos, zero], axis=1), jnp.concatenate([-sin, sin, zero], axis=1))


def _rope_lanes(v, cos_t, sin_t):
    half = QK_ROPE // 2
    lane = lax.broadcasted_iota(I32, v.shape, 1)
    swapped = jnp.where(lane < half, pltpu.roll(v, LANES - half, 1), pltpu.roll(v, half, 1))
    return v * cos_t + swapped * sin_t


def _rms_rows(v, g):
    return v * lax.rsqrt(jnp.mean(v * v, axis=-1, keepdims=True) + RMS_EPS) * g


def _latent_kernel(x_ref, w_ref, qg_ref, kvg_ref, cos_ref, sin_ref, q_ref, kv_ref, kpe_ref, w16_ref,
                   *, q_rank, kv_rank):
    _cast_weight_once(w_ref, w16_ref)
    lat = jnp.dot(x_ref[...], w16_ref[...], preferred_element_type=F32)
    q_ref[...] = _rms_rows(lat[:, :q_rank], qg_ref[...]).astype(BF16)
    kv_ref[...] = _rms_rows(lat[:, q_rank:q_rank + kv_rank], kvg_ref[...]).astype(BF16)
    kpe_ref[...] = _rope_lanes(lat[:, q_rank + kv_rank:], cos_ref[...], sin_ref[...]).astype(BF16)


def _q_up_kernel(a_ref, w_ref, cos_ref, sin_ref, o_ref, w16_ref, *, heads):
    _cast_weight_once(w_ref, w16_ref)
    acc = jnp.dot(a_ref[...], w16_ref[...], preferred_element_type=F32)
    cos_t, sin_t = cos_ref[...], sin_ref[...]
    scale = ATTN_SCALE * LOG2_E
    for h in range(heads):
        lo = h * HEAD_PAD
        o_ref[:, lo:lo + QK_NOPE] = (acc[:, lo:lo + QK_NOPE] * scale).astype(BF16)
        rope = _rope_lanes(acc[:, lo + QK_NOPE:lo + HEAD_PAD], cos_t, sin_t)
        o_ref[:, lo + QK_NOPE:lo + HEAD_PAD] = (rope * scale).astype(BF16)


def _attention_kernel(q_ref, kn_ref, kpe_ref, v_ref, o_ref, k_cat, v_ext, *, key_chunk):
    seq = kn_ref.shape[0]

    @pl.when(pl.program_id(2) == 0)
    def _():
        k_cat[:, :QK_NOPE] = kn_ref[...]
        k_cat[:, QK_NOPE:] = kpe_ref[...]
        v_ext[:, :V_DIM] = v_ref[...]
        v_ext[:, V_DIM:] = jnp.ones((seq, HEAD_PAD - V_DIM), BF16)

    q = q_ref[...]
    chunks = range(0, seq, key_chunk)
    scores = [lax.dot_general(q, k_cat[c:c + key_chunk, :], (((1,), (1,)), ((), ())),
                              preferred_element_type=F32) for c in chunks]
    m = functools.reduce(jnp.maximum, [jnp.max(s, axis=-1, keepdims=True) for s in scores])
    acc = None
    for c, s in zip(chunks, scores):
        p = jnp.exp2(s - m).astype(BF16)
        part = jnp.dot(p, v_ext[c:c + key_chunk, :], preferred_element_type=F32)
        acc = part if acc is None else acc + part
    o_ref[...] = (acc[:, :V_DIM] / acc[:, V_DIM:V_DIM + 1]).astype(BF16)


def _mla_attention(x16, cos_t, sin_t, w_in_p, q_norm, w_uq_p, kv_norm, w_ukv_p, layer, batch, seq):
    n, d = x16.shape
    q_rank, kv_rank = q_norm.shape[1], kv_norm.shape[1]
    heads = w_uq_p.shape[2] // HEAD_PAD
    lat_w = q_rank + kv_rank + LANES

    tm = _tile(n, 512)
    row = lambda w: pl.BlockSpec((tm, w), lambda i: (i, 0))
    q_lat, kv_lat, k_pe = pl.pallas_call(
        functools.partial(_latent_kernel, q_rank=q_rank, kv_rank=kv_rank), grid=(n // tm,),
        in_specs=[row(d), _resident((None, d, lat_w), lambda i: (layer, 0, 0)),
                  pl.BlockSpec((None, 1, q_rank), lambda i: (layer, 0, 0)),
                  pl.BlockSpec((None, 1, kv_rank), lambda i: (layer, 0, 0)),
                  row(LANES), row(LANES)],
        out_specs=(row(q_rank), row(kv_rank), row(LANES)),
        out_shape=(jax.ShapeDtypeStruct((n, q_rank), BF16), jax.ShapeDtypeStruct((n, kv_rank), BF16),
                   jax.ShapeDtypeStruct((n, LANES), BF16)),
        scratch_shapes=[pltpu.VMEM((d, lat_w), BF16)],
        compiler_params=_params(("arbitrary",), 48), name="mla_latent",
    )(x16, w_in_p, q_norm[:, None, :], kv_norm[:, None, :], cos_t, sin_t)

    q_cat = pl.pallas_call(
        functools.partial(_q_up_kernel, heads=heads), grid=(n // tm,),
        in_specs=[row(q_rank), _resident((None, q_rank, heads * HEAD_PAD), lambda i: (layer, 0, 0)),
                  row(LANES), row(LANES)],
        out_specs=row(heads * HEAD_PAD),
        out_shape=jax.ShapeDtypeStruct((n, heads * HEAD_PAD), BF16),
        scratch_shapes=[pltpu.VMEM((q_rank, heads * HEAD_PAD), BF16)],
        compiler_params=_params(("arbitrary",), 56), name="mla_q_up",
    )(q_lat, w_uq_p, cos_t, sin_t)

    kv = _matmul(kv_lat, w_ukv_p, layer, BF16, name="mla_kv_up")

    tq = _tile(seq, 1024)
    kv3 = kv.reshape(batch, seq, -1)
    o = pl.pallas_call(
        functools.partial(_attention_kernel, key_chunk=_tile(seq, 512)), grid=(batch, heads, seq // tq),
        in_specs=[pl.BlockSpec((None, tq, HEAD_PAD), lambda b, h, i: (b, i, h)),
                  pl.BlockSpec((None, seq, QK_NOPE), lambda b, h, i: (b, 0, h)),
                  pl.BlockSpec((None, seq, LANES), lambda b, h, i: (b, 0, 0)),
                  pl.BlockSpec((None, seq, V_DIM), lambda b, h, i: (b, 0, heads + h))],
        out_specs=pl.BlockSpec((None, tq, V_DIM), lambda b, h, i: (b, i, h)),
        out_shape=jax.ShapeDtypeStruct((batch, seq, heads * V_DIM), BF16),
        scratch_shapes=[pltpu.VMEM((seq, HEAD_PAD), BF16), pltpu.VMEM((seq, HEAD_PAD), BF16)],
        compiler_params=_params(("parallel", "parallel", "arbitrary"), 48), name="mla_attention",
    )(q_cat.reshape(batch, seq, -1), kv3, k_pe.reshape(batch, seq, -1), kv3)
    return o.reshape(n, -1)


def _mla_weights(w_in, w_uq, w_ukv):
    layers, q_rank = w_uq.shape[0], w_uq.shape[1]
    kv_rank = w_ukv.shape[1]
    heads = w_uq.shape[2] // (QK_NOPE + QK_ROPE)
    w_in_p = jnp.pad(w_in, ((0, 0), (0, 0), (0, LANES - QK_ROPE)))
    w_uq_p = jnp.pad(w_uq.reshape(layers, q_rank, heads, QK_NOPE + QK_ROPE),
                     ((0, 0), (0, 0), (0, 0), (0, HEAD_PAD - QK_NOPE - QK_ROPE))
                     ).reshape(layers, q_rank, heads * HEAD_PAD)
    w_ukv_p = w_ukv.reshape(layers, kv_rank, heads, 2, QK_NOPE).transpose(0, 1, 3, 2, 4
                                                                            ).reshape(layers, kv_rank, 2 * heads * QK_NOPE)
    return w_in_p, w_uq_p, w_ukv_p


def _swiglu_accumulate(x16, wg_ref, wu_ref, wd_ref, o_ref, rows):
    g = jnp.dot(x16, wg_ref[...].astype(BF16), preferred_element_type=F32)
    u = jnp.dot(x16, wu_ref[...].astype(BF16), preferred_element_type=F32)
    h = (g / (1.0 + jnp.exp(-g)) * u).astype(BF16)
    o_ref[0:rows, :] += jnp.dot(h, wd_ref[...].astype(BF16), preferred_element_type=F32)


def _dense_ffn_kernel(x_ref, wg_ref, wu_ref, wd_ref, o_ref):
    @pl.when(pl.program_id(1) == 0)
    def _():
        o_ref[...] = jnp.zeros_like(o_ref)

    _swiglu_accumulate(x_ref[...], wg_ref, wu_ref, wd_ref, o_ref, x_ref.shape[0])


def _dense_ffn(x16, w_gate, w_up, w_down, layer):
    n, d = x16.shape
    f = w_gate.shape[-1]
    tm, tf = _tile(n, 1024), _tile(f, 256)
    return pl.pallas_call(
        _dense_ffn_kernel, grid=(n // tm, f // tf),
        in_specs=[pl.BlockSpec((tm, d), lambda i, j: (i, 0)),
                  pl.BlockSpec((None, d, tf), lambda i, j: (layer, 0, j)),
                  pl.BlockSpec((None, d, tf), lambda i, j: (layer, 0, j)),
                  pl.BlockSpec((None, tf, d), lambda i, j: (layer, j, 0))],
        out_specs=pl.BlockSpec((tm, d), lambda i, j: (i, 0)),
        out_shape=jax.ShapeDtypeStruct((n, d), F32),
        compiler_params=_params(("parallel", "arbitrary"), 56), name="dense_ffn",
    )(x16, w_gate, w_up, w_down)


def _expert_ffn_kernel(te_ref, nsub_ref, src_ref, x_hbm, wg_ref, wu_ref, wd_ref, o_ref, xbuf, x16, sem, *, chunk):
    i, j = pl.program_id(0), pl.program_id(1)
    n_tiles = pl.num_programs(0)
    tm = o_ref.shape[0]

    def row_copy(r, src_row):
        return pltpu.make_async_copy(x_hbm.at[src_row], xbuf.at[r], sem)

    def issue_rows(tile, lo, count):
        def body(r, c):
            row_copy(lo + r, src_ref[tile * tm + lo + r]).start()
            return c
        lax.fori_loop(0, count, body, 0, unroll=8)

    def wait_rows(lo, count):
        def body(r, c):
            row_copy(lo + r, 0).wait()
            return c
        lax.fori_loop(0, count, body, 0, unroll=8)

    def for_each_sub(n_sub, fn):
        def body(s, c):
            fn(s * MOE_SUB)
            return c
        lax.fori_loop(0, n_sub, body, 0)

    @pl.when(j == 0)
    def _():
        @pl.when(i == 0)
        def _():
            for_each_sub(nsub_ref[0], lambda lo: issue_rows(0, lo, MOE_SUB))
        for_each_sub(nsub_ref[i], lambda lo: wait_rows(lo, MOE_SUB))
        o_ref[...] = jnp.zeros_like(o_ref)
        for n in range(1, tm // MOE_SUB + 1):
            @pl.when(nsub_ref[i] == n)
            def _():
                x16[0:n * MOE_SUB, :] = xbuf[0:n * MOE_SUB, :].astype(BF16)

    nxt = jnp.minimum(i + 1, n_tiles - 1)
    lo = (j - 1) * chunk

    @pl.when((j >= 1) & (i + 1 < n_tiles) & (lo < nsub_ref[nxt] * MOE_SUB))
    def _():
        issue_rows(nxt, lo, chunk)

    for n in range(1, tm // MOE_SUB + 1):
        @pl.when(nsub_ref[i] == n)
        def _():
            rows = n * MOE_SUB
            _swiglu_accumulate(x16[0:rows, :], wg_ref, wu_ref, wd_ref, o_ref, rows)


def _expert_ffn(x32, src, tile_expert, tile_nsub, w_gate, w_up, w_down, layer):
    n_tiles = tile_expert.shape[0]
    d = x32.shape[1]
    f = w_gate.shape[-1]
    tm, tf = MOE_TILE, _tile(f, 256)
    nf = f // tf
    chunk = tm // (1 << int(np.log2(nf - 1)))
    assert MOE_SUB % chunk == 0, "too few hidden-column steps to stream in the next tile"

    def w_col(i, j, te, ns, src):
        return (layer, te[i], 0, jnp.where(ns[i] > 0, j, nf - 1))

    def w_row(i, j, te, ns, src):
        return (layer, te[i], jnp.where(ns[i] > 0, j, nf - 1), 0)

    return pl.pallas_call(
        functools.partial(_expert_ffn_kernel, chunk=chunk),
        grid_spec=pltpu.PrefetchScalarGridSpec(
            num_scalar_prefetch=3, grid=(n_tiles, nf),
            in_specs=[pl.BlockSpec(memory_space=pl.ANY),
                      pl.BlockSpec((None, None, d, tf), w_col), pl.BlockSpec((None, None, d, tf), w_col),
                      pl.BlockSpec((None, None, tf, d), w_row)],
            out_specs=pl.BlockSpec((tm, d), lambda i, j, te, ns, src: (i, 0)),
            scratch_shapes=[pltpu.VMEM((tm, d), F32), pltpu.VMEM((tm, d), BF16),
                            pltpu.SemaphoreType.DMA(())]),
        out_shape=jax.ShapeDtypeStruct((n_tiles * tm, d), F32),
        compiler_params=_params(("arbitrary", "arbitrary"), 56), name="expert_ffn",
    )(tile_expert, tile_nsub, src, x32, w_gate, w_up, w_down)


def _router_kernel(x_ref, w_ref, idx_ref, gate_ref, rank_ref, cnt_ref, carry_ref):
    tm = x_ref.shape[0]
    n_exp = w_ref.shape[0]

    @pl.when(pl.program_id(0) == 0)
    def _():
        carry_ref[...] = jnp.zeros_like(carry_ref)

    x = x_ref[...]
    xh = x.astype(BF16)
    xl = (x - xh.astype(F32)).astype(BF16)
    w = w_ref[...]
    wh = w.astype(BF16)
    wl = (w - wh.astype(F32)).astype(BF16)
    nt_dims = (((1,), (1,)), ((), ()))
    logits = (lax.dot_general(wh, xh, nt_dims, preferred_element_type=F32)
              + lax.dot_general(wh, xl, nt_dims, preferred_element_type=F32)
              + lax.dot_general(wl, xh, nt_dims, preferred_element_type=F32))

    e_id = lax.broadcasted_iota(I32, (n_exp, tm), 0).astype(F32)
    m1 = jnp.max(logits, axis=0, keepdims=True)
    i1 = jnp.min(jnp.where(logits == m1, e_id, float(n_exp)), axis=0, keepdims=True)
    first = e_id == i1
    rest = jnp.where(first, -jnp.inf, logits)
    m2 = jnp.max(rest, axis=0, keepdims=True)
    i2 = jnp.min(jnp.where(rest == m2, e_id, float(n_exp)), axis=0, keepdims=True)
    second = e_id == i2
    d = jnp.exp(m2 - m1)
    g1 = 1.0 / (1.0 + d)

    chosen = jnp.where(first | second, 1.0, 0.0)
    before = (lax.broadcasted_iota(I32, (tm, tm), 0) < lax.broadcasted_iota(I32, (tm, tm), 1))
    rank = jnp.dot(chosen.astype(BF16), jnp.where(before, 1.0, 0.0).astype(BF16),
                   preferred_element_type=F32) + carry_ref[:, :1]
    r1 = jnp.sum(jnp.where(first, rank, 0.0), axis=0, keepdims=True)
    r2 = jnp.sum(jnp.where(second, rank, 0.0), axis=0, keepdims=True)
    carry_ref[...] = carry_ref[...] + jnp.sum(chosen, axis=1, keepdims=True)

    idx_ref[0:1, :] = i1.astype(I32)
    idx_ref[1:2, :] = i2.astype(I32)
    gate_ref[0:1, :] = g1
    gate_ref[1:2, :] = d * g1
    rank_ref[0:1, :] = r1.astype(I32)
    rank_ref[1:2, :] = r2.astype(I32)
    cnt_ref[...] = carry_ref[...].astype(I32)


def _route(x32, w_router_t, layer):
    n, d = x32.shape
    n_exp = w_router_t.shape[1]
    tm = _tile(n, 1024)
    slot = pl.BlockSpec((TOP_K, tm), lambda i: (0, i))
    return pl.pallas_call(
        _router_kernel, grid=(n // tm,),
        in_specs=[pl.BlockSpec((tm, d), lambda i: (i, 0)),
                  pl.BlockSpec((None, n_exp, d), lambda i: (layer, 0, 0))],
        out_specs=(slot, slot, slot, pl.BlockSpec((n_exp, LANES), lambda i: (0, 0))),
        out_shape=(jax.ShapeDtypeStruct((TOP_K, n), I32), jax.ShapeDtypeStruct((TOP_K, n), F32),
                   jax.ShapeDtypeStruct((TOP_K, n), I32), jax.ShapeDtypeStruct((n_exp, LANES), I32)),
        scratch_shapes=[pltpu.VMEM((n_exp, LANES), F32)],
        compiler_params=_params(("arbitrary",), 40), name="router",
    )(x32, w_router_t)


def _combine_kernel(pos_ref, x_ref, gate_ref, g_ref, b_ref, y_hbm, o32_ref, o16_ref, buf, sem, *, alpha, n_tok):
    rows = x_ref.shape[0]
    base = pl.program_id(0) * rows

    def row_copy(k, r, src_row):
        return pltpu.make_async_copy(y_hbm.at[src_row], buf.at[k, r], sem)

    def issue(r, c):
        for k in range(TOP_K):
            row_copy(k, r, pos_ref[k * n_tok + base + r]).start()
        return c

    def drain(r, c):
        for k in range(TOP_K):
            row_copy(k, r, 0).wait()
        return c

    lax.fori_loop(0, rows, issue, 0, unroll=8)
    lax.fori_loop(0, rows, drain, 0, unroll=8)
    gates = gate_ref[...]
    y = gates[:, 0:1] * buf[0] + gates[:, 1:2] * buf[1]
    out = _ln_rows(alpha * x_ref[...] + y, g_ref[...], b_ref[...])
    o32_ref[...] = out
    o16_ref[...] = out.astype(BF16)


def _combine(pos, gates_t, x32, y_sorted, g, b, alpha):
    n, d = x32.shape
    tc = _tile(n, 256)
    row = pl.BlockSpec((tc, d), lambda i, p: (i, 0))
    vec = pl.BlockSpec((1, d), lambda i, p: (0, 0))
    return pl.pallas_call(
        functools.partial(_combine_kernel, alpha=alpha, n_tok=n),
        grid_spec=pltpu.PrefetchScalarGridSpec(
            num_scalar_prefetch=1, grid=(n // tc,),
            in_specs=[row, pl.BlockSpec((tc, TOP_K), lambda i, p: (i, 0)), vec, vec,
                      pl.BlockSpec(memory_space=pl.ANY)],
            out_specs=(row, row),
            scratch_shapes=[pltpu.VMEM((TOP_K, tc, d), F32), pltpu.SemaphoreType.DMA(())]),
        out_shape=(jax.ShapeDtypeStruct((n, d), F32), jax.ShapeDtypeStruct((n, d), BF16)),
        compiler_params=_params(("arbitrary",), 40), name="moe_combine",
    )(pos.reshape(-1), x32, gates_t, g.reshape(1, d), b.reshape(1, d), y_sorted)


def _moe_layer(x32, w_router_t, w_gate, w_up, w_down, layer, ln_g, ln_b, alpha):
    n, d = x32.shape
    n_exp = w_router_t.shape[1]
    idx, gates, rank, cnt = _route(x32, w_router_t, layer)

    n_tiles = (TOP_K * n) // MOE_TILE + n_exp
    counts = cnt[:, 0]
    tiles_e = (counts + MOE_TILE - 1) // MOE_TILE
    tile_end = jnp.cumsum(tiles_e)
    tile_start = tile_end - tiles_e
    pos = rank + sum(jnp.where(idx == e, tile_start[e] * MOE_TILE, 0) for e in range(n_exp))
    tile_id = jnp.arange(n_tiles, dtype=I32)
    used = tile_id < tile_end[-1]
    owner = jnp.minimum(jnp.sum((tile_id[:, None] >= tile_end[None, :]).astype(I32), axis=1), n_exp - 1)
    last_owner = jnp.max(jnp.where(used, owner, 0))
    tile_expert = jnp.where(used, owner, last_owner)
    rows_in_tile = jnp.clip(counts[owner] - (tile_id - tile_start[owner]) * MOE_TILE, 0, MOE_TILE)
    tile_nsub = jnp.where(used, (rows_in_tile + MOE_SUB - 1) // MOE_SUB, 0).astype(I32)
    tok = jnp.tile(jnp.arange(n, dtype=I32), TOP_K)
    src = jnp.zeros((n_tiles * MOE_TILE,), I32).at[pos.reshape(-1)].set(tok)

    y_sorted = _expert_ffn(x32, src, tile_expert, tile_nsub, w_gate, w_up, w_down, layer)
    return _combine(pos, gates.T, x32, y_sorted, ln_g, ln_b, alpha)


def kernel(x, positions, emb_ln_g, emb_ln_b, ln_g, ln_b, fourier_w_o, mla_w_in, mla_q_norm, mla_w_uq, mla_kv_norm, mla_w_ukv, mla_w_o, ffn_w_gate, ffn_w_up, ffn_w_down, moe_w_router, moe_w_gate, moe_w_up, moe_w_down):
    batch, seq, d = x.shape
    depth = ln_g.shape[0]
    alpha = (2.0 * depth) ** 0.25
    cos_t, sin_t = _rope_tables(positions)
    dft_tabs = _fourier_tables(seq)
    w_in_p, w_uq_p, w_ukv_p = _mla_weights(mla_w_in, mla_w_uq, mla_w_ukv)
    w_router_t = moe_w_router.transpose(0, 2, 1)

    x32, x16 = _layer_norm(x.reshape(batch * seq, d), emb_ln_g, emb_ln_b)
    for i in range(depth):
        j = i // 2
        if i % 2 == 0:
            mixed = _fourier_mixed(x16, dft_tabs, batch, seq)
            x32, x16 = _matmul_add_ln(mixed, fourier_w_o, j, x32, ln_g[i, 0], ln_b[i, 0], alpha, "fourier_out_ln")
            f = _dense_ffn(x16, ffn_w_gate, ffn_w_up, ffn_w_down, j)
            x32, x16 = _layer_norm(x32, ln_g[i, 1], ln_b[i, 1], f, alpha)
        else:
            attn = _mla_attention(x16, cos_t, sin_t, w_in_p, mla_q_norm, w_uq_p, mla_kv_norm, w_ukv_p, j, batch, seq)
            x32, x16 = _matmul_add_ln(attn, mla_w_o, j, x32, ln_g[i, 0], ln_b[i, 0], alpha, "mla_out_ln")
            x32, x16 = _moe_layer(x32, w_router_t, moe_w_gate, moe_w_up, moe_w_down, j,
                                  ln_g[i, 1], ln_b[i, 1], alpha)
    return x32.reshape(batch, seq, d)
```

```python
import functools

import numpy as np
import jax
import jax.numpy as jnp
from jax import lax
from jax.experimental import pallas as pl
from jax.experimental.pallas import tpu as pltpu

F32, BF16, I32 = jnp.float32, jnp.bfloat16, jnp.int32

GROUP_DIM = 256
QK_NOPE = 128
QK_ROPE = 64
V_DIM = 128
HEAD_PAD = 256
TOP_K = 2
ROPE_THETA = 10000.0
LN_EPS = 1e-5
RMS_EPS = 1e-6
ATTN_SCALE = (QK_NOPE + QK_ROPE) ** -0.5
LOG2_E = 1.4426950408889634

LANES = 128
MIB = 1024 * 1024

MOE_TILE = 1024
MOE_SUB = 256


def _params(semantics, vmem_mib, flags=None):
    return pltpu.CompilerParams(dimension_semantics=semantics, vmem_limit_bytes=vmem_mib * MIB, flags=flags)


def _tile(n, pref):
    t = min(n, pref)
    while n % t:
        t //= 2
    return t


def _resident(block_shape, index_map):
    return pl.BlockSpec(block_shape, index_map, pipeline_mode=pl.Buffered(1))


def _ln_rows(v, g, b):
    mu = jnp.mean(v, axis=-1, keepdims=True)
    vc = v - mu
    var = jnp.mean(vc * vc, axis=-1, keepdims=True)
    return vc * lax.rsqrt(var + LN_EPS) * g + b


def _ln_kernel(x_ref, g_ref, b_ref, o32_ref, o16_ref):
    y = _ln_rows(x_ref[...], g_ref[...], b_ref[...])
    o32_ref[...] = y
    o16_ref[...] = y.astype(BF16)


def _add_ln_kernel(x_ref, h_ref, g_ref, b_ref, o32_ref, o16_ref, *, alpha):
    y = _ln_rows(alpha * x_ref[...] + h_ref[...], g_ref[...], b_ref[...])
    o32_ref[...] = y
    o16_ref[...] = y.astype(BF16)


def _layer_norm(x, g, b, h=None, alpha=1.0):
    n, d = x.shape
    tm = _tile(n, 512)
    row = pl.BlockSpec((tm, d), lambda i: (i, 0))
    vec = pl.BlockSpec((1, d), lambda i: (0, 0))
    g2, b2 = g.reshape(1, d), b.reshape(1, d)
    out_shape = (jax.ShapeDtypeStruct((n, d), F32), jax.ShapeDtypeStruct((n, d), BF16))
    if h is None:
        return pl.pallas_call(
            _ln_kernel, grid=(n // tm,), in_specs=[row, vec, vec], out_specs=(row, row),
            out_shape=out_shape, compiler_params=_params(("parallel",), 40), name="ln",
        )(x, g2, b2)
    return pl.pallas_call(
        functools.partial(_add_ln_kernel, alpha=alpha), grid=(n // tm,),
        in_specs=[row, row, vec, vec], out_specs=(row, row),
        out_shape=out_shape, compiler_params=_params(("parallel",), 40), name="add_ln",
    )(x, h, g2, b2)


def _cast_weight_once(w_ref, w16_ref):
    @pl.when(pl.program_id(0) == 0)
    def _():
        w16_ref[...] = w_ref[...].astype(BF16)


def _mm_kernel(a_ref, w_ref, o_ref, w16_ref):
    _cast_weight_once(w_ref, w16_ref)
    o_ref[...] = jnp.dot(a_ref[...], w16_ref[...], preferred_element_type=F32).astype(o_ref.dtype)


def _matmul(a, w_stack, layer, out_dtype, tm_pref=512, name="mm"):
    m, k = a.shape
    n = w_stack.shape[2]
    tm = _tile(m, tm_pref)
    return pl.pallas_call(
        _mm_kernel, grid=(m // tm,),
        in_specs=[pl.BlockSpec((tm, k), lambda i: (i, 0)),
                  _resident((None, k, n), lambda i: (layer, 0, 0))],
        out_specs=pl.BlockSpec((tm, n), lambda i: (i, 0)),
        out_shape=jax.ShapeDtypeStruct((m, n), out_dtype),
        scratch_shapes=[pltpu.VMEM((k, n), BF16)],
        compiler_params=_params(("arbitrary",), 56), name=name,
    )(a, w_stack)


def _mm_add_ln_kernel(a_ref, w_ref, x_ref, g_ref, b_ref, o32_ref, o16_ref, w16_ref, *, alpha):
    _cast_weight_once(w_ref, w16_ref)
    h = jnp.dot(a_ref[...], w16_ref[...], preferred_element_type=F32)
    y = _ln_rows(alpha * x_ref[...] + h, g_ref[...], b_ref[...])
    o32_ref[...] = y
    o16_ref[...] = y.astype(BF16)


def _matmul_add_ln(a, w_stack, layer, x32, g, b, alpha, name):
    m, k = a.shape
    d = w_stack.shape[2]
    tm = _tile(m, 256)
    row = lambda w: pl.BlockSpec((tm, w), lambda i: (i, 0))
    vec = pl.BlockSpec((1, d), lambda i: (0, 0))
    return pl.pallas_call(
        functools.partial(_mm_add_ln_kernel, alpha=alpha), grid=(m // tm,),
        in_specs=[row(k), _resident((None, k, d), lambda i: (layer, 0, 0)), row(d), vec, vec],
        out_specs=(row(d), row(d)),
        out_shape=(jax.ShapeDtypeStruct((m, d), F32), jax.ShapeDtypeStruct((m, d), BF16)),
        scratch_shapes=[pltpu.VMEM((k, d), BF16)],
        compiler_params=_params(("arbitrary",), 56), name=name,
    )(a, w_stack, x32, g.reshape(1, d), b.reshape(1, d))


def _dft_tables(n, scale):
    n0 = 1 << (int(np.log2(n)) // 2)
    s = jnp.arange(n, dtype=I32)[None, :]
    step = np.float32(2.0 * np.pi / n)
    hi = ((jnp.arange(n // n0, dtype=I32)[:, None] * n0 * s) % n).astype(F32) * step
    lo = ((jnp.arange(n0, dtype=I32)[:, None] * s) % n).astype(F32) * step
    ch, sh = jnp.cos(hi)[:, None, :], jnp.sin(hi)[:, None, :]
    cl, sl = jnp.cos(lo)[None, :, :], jnp.sin(lo)[None, :, :]
    cos = (ch * cl - sh * sl).reshape(n, n)
    sin = (sh * cl + ch * sl).reshape(n, n)
    return cos * np.float32(scale), sin * np.float32(scale)


def _fourier_tables(seq):
    cg, sg = _dft_tables(GROUP_DIM, GROUP_DIM ** -0.5)
    cs, ss = _dft_tables(seq, seq ** -0.5)
    return (jnp.concatenate([cg, sg], axis=1).astype(BF16),
            jnp.concatenate([cs, -ss], axis=1).astype(BF16))


def _group_dft_kernel(x_ref, t_ref, z_ref, *, groups):
    t = t_ref[...]
    for g in range(groups):
        cols = slice(g * GROUP_DIM, (g + 1) * GROUP_DIM)
        z = jnp.dot(x_ref[:, cols], t, preferred_element_type=F32)
        z_ref[0, :, cols] = z[:, :GROUP_DIM].astype(BF16)
        z_ref[1, :, cols] = z[:, GROUP_DIM:].astype(BF16)


def _seq_dft_kernel(t_ref, z_ref, y_ref):
    y_ref[...] = jnp.dot(t_ref[...], z_ref[...], preferred_element_type=F32).astype(BF16)


def _fourier_mixed(x16, tabs, batch, seq):
    n, d = x16.shape
    t_group, t_seq = tabs
    tm = _tile(seq, 512)
    z = pl.pallas_call(
        functools.partial(_group_dft_kernel, groups=d // GROUP_DIM),
        grid=(batch, seq // tm),
        in_specs=[pl.BlockSpec((None, tm, d), lambda b, i: (b, i, 0)),
                  pl.BlockSpec((GROUP_DIM, 2 * GROUP_DIM), lambda b, i: (0, 0))],
        out_specs=pl.BlockSpec((None, 2, tm, d), lambda b, i: (b, 0, i, 0)),
        out_shape=jax.ShapeDtypeStruct((batch, 2, seq, d), BF16),
        compiler_params=_params(("parallel", "parallel"), 40), name="group_dft",
    )(x16.reshape(batch, seq, d), t_group)
    tr, tn = _tile(seq, 512), _tile(d, 512)
    y = pl.pallas_call(
        _seq_dft_kernel, grid=(seq // tr, batch, d // tn),
        in_specs=[pl.BlockSpec((tr, 2 * seq), lambda r, b, j: (r, 0)),
                  pl.BlockSpec((None, 2 * seq, tn), lambda r, b, j: (b, 0, j))],
        out_specs=pl.BlockSpec((None, tr, tn), lambda r, b, j: (b, r, j)),
        out_shape=jax.ShapeDtypeStruct((batch, seq, d), BF16),
        compiler_params=_params(("parallel", "parallel", "parallel"), 40), name="seq_dft",
    )(t_seq, z.reshape(batch, 2 * seq, d))
    return y.reshape(n, d)


def _rope_tables(positions):
    inv_freq = ROPE_THETA ** (-jnp.arange(0, QK_ROPE, 2, dtype=F32) / QK_ROPE)
    ang = positions.astype(F32).reshape(-1, 1) * inv_freq
    cos, sin = jnp.cos(ang), jnp.sin(ang)
    zero = jnp.zeros((ang.shape[0], LANES - QK_ROPE), F32)
    return (jnp.concatenate([cos, cos, zero], axis=1), jnp.concatenate([-sin, sin, zero], axis=1))


def _rope_lanes(v, cos_t, sin_t):
    half = QK_ROPE // 2
    lane = lax.broadcasted_iota(I32, v.shape, 1)
    swapped = jnp.where(lane < half, pltpu.roll(v, LANES - half, 1), pltpu.roll(v, half, 1))
    return v * cos_t + swapped * sin_t


def _rms_rows(v, g):
    return v * lax.rsqrt(jnp.mean(v * v, axis=-1, keepdims=True) + RMS_EPS) * g


def _latent_kernel(x_ref, w_ref, qg_ref, kvg_ref, cos_ref, sin_ref, q_ref, kv_ref, kpe_ref, w16_ref,
                   *, q_rank, kv_rank):
    _cast_weight_once(w_ref, w16_ref)
    lat = jnp.dot(x_ref[...], w16_ref[...], preferred_element_type=F32)
    q_ref[...] = _rms_rows(lat[:, :q_rank], qg_ref[...]).astype(BF16)
    kv_ref[...] = _rms_rows(lat[:, q_rank:q_rank + kv_rank], kvg_ref[...]).astype(BF16)
    kpe_ref[...] = _rope_lanes(_pad_rope(lat[:, q_rank + kv_rank:]), cos_ref[...], sin_ref[...]).astype(BF16)


def _pad_rope(v):
    return jnp.concatenate([v, jnp.zeros((v.shape[0], LANES - QK_ROPE), v.dtype)], axis=1)


def _q_up_kernel(a_ref, w_ref, cos_ref, sin_ref, o_ref, w16_ref, *, heads):
    _cast_weight_once(w_ref, w16_ref)
    acc = jnp.dot(a_ref[...], w16_ref[...], preferred_element_type=F32)
    cos_t, sin_t = cos_ref[...], sin_ref[...]
    scale = ATTN_SCALE * LOG2_E
    for h in range(heads):
        src, dst = h * (QK_NOPE + QK_ROPE), h * HEAD_PAD
        o_ref[:, dst:dst + QK_NOPE] = (acc[:, src:src + QK_NOPE] * scale).astype(BF16)
        rope = _rope_lanes(_pad_rope(acc[:, src + QK_NOPE:src + QK_NOPE + QK_ROPE]), cos_t, sin_t)
        o_ref[:, dst + QK_NOPE:dst + HEAD_PAD] = (rope * scale).astype(BF16)


def _attention_kernel(q_ref, kn_ref, kpe_ref, v_ref, o_ref, k_cat, v_ext, s_even, s_odd, m_even, m_odd,
                      *, key_chunk, q_tiles):
    t = pl.program_id(0)
    n_items = pl.num_programs(0) - 1
    seq = kn_ref.shape[0]

    @pl.when(t == 0)
    def _():
        s_odd[...] = jnp.zeros(s_odd.shape, F32)
        m_odd[...] = jnp.zeros(m_odd.shape, F32)
        v_ext[...] = jnp.ones(v_ext.shape, BF16)

    @pl.when((t % q_tiles == 0) & (t < n_items))
    def _():
        k_cat[:, :QK_NOPE] = kn_ref[...]
        k_cat[:, QK_NOPE:] = kpe_ref[...]

    @pl.when(((t - 1) % q_tiles == 0) & (t >= 1))
    def _():
        v_ext[:, :V_DIM] = v_ref[...]
        v_ext[:, V_DIM:] = jnp.ones((seq, HEAD_PAD - V_DIM), BF16)

    chunks = range(0, seq, key_chunk)

    def stages(s_cur, m_cur, s_prev, m_prev):
        q = q_ref[...]
        m_lanes = None
        for c in chunks:
            s = lax.dot_general(q, k_cat[c:c + key_chunk, :], (((1,), (1,)), ((), ())),
                                preferred_element_type=F32)
            s_cur[:, c:c + key_chunk] = s
            for l in range(0, key_chunk, LANES):
                m_lanes = s[:, l:l + LANES] if m_lanes is None else jnp.maximum(m_lanes, s[:, l:l + LANES])
        m_cur[...] = jnp.broadcast_to(jnp.max(m_lanes, axis=-1, keepdims=True), m_cur.shape)

        m_old = jnp.concatenate([m_prev[...]] * (key_chunk // LANES), axis=1)
        acc = None
        for c in chunks:
            p = jnp.exp2(s_prev[:, c:c + key_chunk] - m_old).astype(BF16)
            part = jnp.dot(p, v_ext[c:c + key_chunk, :], preferred_element_type=F32)
            acc = part if acc is None else acc + part
        o_ref[...] = (acc[:, :V_DIM] / acc[:, V_DIM:]).astype(BF16)

    @pl.when(t % 2 == 0)
    def _():
        stages(s_even, m_even, s_odd, m_odd)

    @pl.when(t % 2 == 1)
    def _():
        stages(s_odd, m_odd, s_even, m_even)


def _mla_attention(x16, cos_t, sin_t, w_in, q_norm, w_uq, kv_norm, w_ukv, layer, batch, seq):
    n, d = x16.shape
    q_rank, kv_rank = q_norm.shape[1], kv_norm.shape[1]
    heads = w_uq.shape[2] // (QK_NOPE + QK_ROPE)
    lat_w = q_rank + kv_rank + QK_ROPE

    tm = _tile(n, 512)
    row = lambda w: pl.BlockSpec((tm, w), lambda i: (i, 0))
    q_lat, kv_lat, k_pe = pl.pallas_call(
        functools.partial(_latent_kernel, q_rank=q_rank, kv_rank=kv_rank), grid=(n // tm,),
        in_specs=[row(d), _resident((None, d, lat_w), lambda i: (layer, 0, 0)),
                  pl.BlockSpec((None, 1, q_rank), lambda i: (layer, 0, 0)),
                  pl.BlockSpec((None, 1, kv_rank), lambda i: (layer, 0, 0)),
                  row(LANES), row(LANES)],
        out_specs=(row(q_rank), row(kv_rank), row(LANES)),
        out_shape=(jax.ShapeDtypeStruct((n, q_rank), BF16), jax.ShapeDtypeStruct((n, kv_rank), BF16),
                   jax.ShapeDtypeStruct((n, LANES), BF16)),
        scratch_shapes=[pltpu.VMEM((d, lat_w), BF16)],
        compiler_params=_params(("arbitrary",), 48), name="mla_latent",
    )(x16, w_in, q_norm[:, None, :], kv_norm[:, None, :], cos_t, sin_t)

    q_cols = heads * (QK_NOPE + QK_ROPE)
    q_cat = pl.pallas_call(
        functools.partial(_q_up_kernel, heads=heads), grid=(n // tm,),
        in_specs=[row(q_rank), _resident((None, q_rank, q_cols), lambda i: (layer, 0, 0)),
                  row(LANES), row(LANES)],
        out_specs=row(heads * HEAD_PAD),
        out_shape=jax.ShapeDtypeStruct((n, heads * HEAD_PAD), BF16),
        scratch_shapes=[pltpu.VMEM((q_rank, q_cols), BF16)],
        compiler_params=_params(("arbitrary",), 56), name="mla_q_up",
    )(q_lat, w_uq, cos_t, sin_t)

    kv = _matmul(kv_lat, w_ukv, layer, BF16, name="mla_kv_up")

    tq = _tile(seq, 1024)
    q_tiles = seq // tq
    n_items = batch * heads * q_tiles

    def item(t):
        t = jnp.clip(t, 0, n_items - 1)
        return t // (heads * q_tiles), (t // q_tiles) % heads, t % q_tiles

    def score_q(t):
        b, h, i = item(t)
        return b, i, h

    def score_k(t):
        b, h, _ = item(t)
        return b, 0, 2 * h

    def value_v(t):
        b, h, _ = item(t - 1)
        return b, 0, 2 * h + 1

    def value_o(t):
        b, h, i = item(t - 1)
        return b, i, h

    kv3 = kv.reshape(batch, seq, -1)
    o = pl.pallas_call(
        functools.partial(_attention_kernel, key_chunk=_tile(seq, 512), q_tiles=q_tiles), grid=(n_items + 1,),
        in_specs=[pl.BlockSpec((None, tq, HEAD_PAD), score_q),
                  pl.BlockSpec((None, seq, QK_NOPE), score_k),
                  pl.BlockSpec((None, seq, LANES), lambda t: (item(t)[0], 0, 0)),
                  pl.BlockSpec((None, seq, V_DIM), value_v)],
        out_specs=pl.BlockSpec((None, tq, V_DIM), value_o),
        out_shape=jax.ShapeDtypeStruct((batch, seq, heads * V_DIM), BF16),
        scratch_shapes=[pltpu.VMEM((seq, HEAD_PAD), BF16), pltpu.VMEM((seq, HEAD_PAD), BF16),
                        pltpu.VMEM((tq, seq), F32), pltpu.VMEM((tq, seq), F32),
                        pltpu.VMEM((tq, LANES), F32), pltpu.VMEM((tq, LANES), F32)],
        compiler_params=_params(("arbitrary",), 56), name="mla_attention",
    )(q_cat.reshape(batch, seq, -1), kv3, k_pe.reshape(batch, seq, -1), kv3)
    return o.reshape(n, -1)


def _swiglu_accumulate(x16, wg_ref, wu_ref, wd_ref, o_ref, rows):
    g = jnp.dot(x16, wg_ref[...].astype(BF16), preferred_element_type=F32)
    u = jnp.dot(x16, wu_ref[...].astype(BF16), preferred_element_type=F32)
    h = (g / (1.0 + jnp.exp(-g)) * u).astype(BF16)
    o_ref[0:rows, :] += jnp.dot(h, wd_ref[...].astype(BF16), preferred_element_type=F32)


def _dense_ffn_kernel(x_ref, wg_ref, wu_ref, wd_ref, o_ref):
    @pl.when(pl.program_id(1) == 0)
    def _():
        o_ref[...] = jnp.zeros_like(o_ref)

    _swiglu_accumulate(x_ref[...], wg_ref, wu_ref, wd_ref, o_ref, x_ref.shape[0])


def _dense_ffn(x16, w_gate, w_up, w_down, layer):
    n, d = x16.shape
    f = w_gate.shape[-1]
    tm, tf = _tile(n, 1024), _tile(f, 256)
    return pl.pallas_call(
        _dense_ffn_kernel, grid=(n // tm, f // tf),
        in_specs=[pl.BlockSpec((tm, d), lambda i, j: (i, 0)),
                  pl.BlockSpec((None, d, tf), lambda i, j: (layer, 0, j)),
                  pl.BlockSpec((None, d, tf), lambda i, j: (layer, 0, j)),
                  pl.BlockSpec((None, tf, d), lambda i, j: (layer, j, 0))],
        out_specs=pl.BlockSpec((tm, d), lambda i, j: (i, 0)),
        out_shape=jax.ShapeDtypeStruct((n, d), F32),
        compiler_params=_params(("parallel", "arbitrary"), 56), name="dense_ffn",
    )(x16, w_gate, w_up, w_down)


def _expert_ffn_kernel(te_ref, nsub_ref, src_ref, x_hbm, wg_ref, wu_ref, wd_ref, o_ref, xbuf, x16, sem, *, chunk):
    i, j = pl.program_id(0), pl.program_id(1)
    n_tiles = pl.num_programs(0)
    tm = o_ref.shape[0]

    def row_copy(r, src_row):
        return pltpu.make_async_copy(x_hbm.at[src_row], xbuf.at[r], sem)

    def issue_rows(tile, lo, count):
        def body(r, c):
            row_copy(lo + r, src_ref[tile * tm + lo + r]).start()
            return c
        lax.fori_loop(0, count, body, 0, unroll=8)

    def wait_rows(lo, count):
        def body(r, c):
            row_copy(lo + r, 0).wait()
            return c
        lax.fori_loop(0, count, body, 0, unroll=8)

    def for_each_sub(n_sub, fn):
        def body(s, c):
            fn(s * MOE_SUB)
            return c
        lax.fori_loop(0, n_sub, body, 0)

    @pl.when(j == 0)
    def _():
        @pl.when(i == 0)
        def _():
            for_each_sub(nsub_ref[0], lambda lo: issue_rows(0, lo, MOE_SUB))
        for_each_sub(nsub_ref[i], lambda lo: wait_rows(lo, MOE_SUB))
        o_ref[...] = jnp.zeros_like(o_ref)
        for n in range(1, tm // MOE_SUB + 1):
            @pl.when(nsub_ref[i] == n)
            def _():
                x16[0:n * MOE_SUB, :] = xbuf[0:n * MOE_SUB, :].astype(BF16)

    nxt = jnp.minimum(i + 1, n_tiles - 1)
    lo = (j - 1) * chunk

    @pl.when((j >= 1) & (i + 1 < n_tiles) & (lo < nsub_ref[nxt] * MOE_SUB))
    def _():
        issue_rows(nxt, lo, chunk)

    for n in range(1, tm // MOE_SUB + 1):
        @pl.when(nsub_ref[i] == n)
        def _():
            rows = n * MOE_SUB
            _swiglu_accumulate(x16[0:rows, :], wg_ref, wu_ref, wd_ref, o_ref, rows)


def _expert_ffn(x32, src, tile_expert, tile_nsub, w_gate, w_up, w_down, layer):
    n_tiles = tile_expert.shape[0]
    d = x32.shape[1]
    f = w_gate.shape[-1]
    tm, tf = MOE_TILE, _tile(f, 256)
    nf = f // tf
    chunk = tm // (1 << int(np.log2(nf - 1)))
    assert MOE_SUB % chunk == 0, "too few hidden-column steps to stream in the next tile"

    def w_col(i, j, te, ns, src):
        return (layer, te[i], 0, jnp.where(ns[i] > 0, j, nf - 1))

    def w_row(i, j, te, ns, src):
        return (layer, te[i], jnp.where(ns[i] > 0, j, nf - 1), 0)

    return pl.pallas_call(
        functools.partial(_expert_ffn_kernel, chunk=chunk),
        grid_spec=pltpu.PrefetchScalarGridSpec(
            num_scalar_prefetch=3, grid=(n_tiles, nf),
            in_specs=[pl.BlockSpec(memory_space=pl.ANY),
                      pl.BlockSpec((None, None, d, tf), w_col), pl.BlockSpec((None, None, d, tf), w_col),
                      pl.BlockSpec((None, None, tf, d), w_row)],
            out_specs=pl.BlockSpec((tm, d), lambda i, j, te, ns, src: (i, 0)),
            scratch_shapes=[pltpu.VMEM((tm, d), F32), pltpu.VMEM((tm, d), BF16),
                            pltpu.SemaphoreType.DMA(())]),
        out_shape=jax.ShapeDtypeStruct((n_tiles * tm, d), F32),
        compiler_params=_params(("arbitrary", "arbitrary"), 56), name="expert_ffn",
    )(tile_expert, tile_nsub, src, x32, w_gate, w_up, w_down)


def _router_kernel(x_ref, w_ref, idx_ref, gate_ref, rank_ref, cnt_ref, carry_ref):
    tm = x_ref.shape[0]
    n_exp = w_ref.shape[0]

    @pl.when(pl.program_id(0) == 0)
    def _():
        carry_ref[...] = jnp.zeros_like(carry_ref)

    x = x_ref[...]
    xh = x.astype(BF16)
    xl = (x - xh.astype(F32)).astype(BF16)
    w = w_ref[...]
    wh = w.astype(BF16)
    wl = (w - wh.astype(F32)).astype(BF16)
    nt_dims = (((1,), (1,)), ((), ()))
    logits = (lax.dot_general(wh, xh, nt_dims, preferred_element_type=F32)
              + lax.dot_general(wh, xl, nt_dims, preferred_element_type=F32)
              + lax.dot_general(wl, xh, nt_dims, preferred_element_type=F32))

    e_id = lax.broadcasted_iota(I32, (n_exp, tm), 0).astype(F32)
    m1 = jnp.max(logits, axis=0, keepdims=True)
    i1 = jnp.min(jnp.where(logits == m1, e_id, float(n_exp)), axis=0, keepdims=True)
    first = e_id == i1
    rest = jnp.where(first, -jnp.inf, logits)
    m2 = jnp.max(rest, axis=0, keepdims=True)
    i2 = jnp.min(jnp.where(rest == m2, e_id, float(n_exp)), axis=0, keepdims=True)
    second = e_id == i2
    d = jnp.exp(m2 - m1)
    g1 = 1.0 / (1.0 + d)

    chosen = jnp.where(first | second, 1.0, 0.0)
    before = (lax.broadcasted_iota(I32, (tm, tm), 0) < lax.broadcasted_iota(I32, (tm, tm), 1))
    rank = jnp.dot(chosen.astype(BF16), jnp.where(before, 1.0, 0.0).astype(BF16),
                   preferred_element_type=F32) + carry_ref[:, :1]
    r1 = jnp.sum(jnp.where(first, rank, 0.0), axis=0, keepdims=True)
    r2 = jnp.sum(jnp.where(second, rank, 0.0), axis=0, keepdims=True)
    carry_ref[...] = carry_ref[...] + jnp.sum(chosen, axis=1, keepdims=True)

    idx_ref[0:1, :] = i1.astype(I32)
    idx_ref[1:2, :] = i2.astype(I32)
    gate_ref[0:1, :] = g1
    gate_ref[1:2, :] = d * g1
    rank_ref[0:1, :] = r1.astype(I32)
    rank_ref[1:2, :] = r2.astype(I32)
    cnt_ref[...] = carry_ref[...].astype(I32)


def _route(x32, w_router_t, layer):
    n, d = x32.shape
    n_exp = w_router_t.shape[1]
    tm = _tile(n, 1024)
    slot = pl.BlockSpec((TOP_K, tm), lambda i: (0, i))
    return pl.pallas_call(
        _router_kernel, grid=(n // tm,),
        in_specs=[pl.BlockSpec((tm, d), lambda i: (i, 0)),
                  pl.BlockSpec((None, n_exp, d), lambda i: (layer, 0, 0))],
        out_specs=(slot, slot, slot, pl.BlockSpec((n_exp, LANES), lambda i: (0, 0))),
        out_shape=(jax.ShapeDtypeStruct((TOP_K, n), I32), jax.ShapeDtypeStruct((TOP_K, n), F32),
                   jax.ShapeDtypeStruct((TOP_K, n), I32), jax.ShapeDtypeStruct((n_exp, LANES), I32)),
        scratch_shapes=[pltpu.VMEM((n_exp, LANES), F32)],
        compiler_params=_params(("arbitrary",), 40), name="router",
    )(x32, w_router_t)


def _combine_kernel(pos_ref, x_ref, gate_ref, g_ref, b_ref, y_hbm, o32_ref, o16_ref, buf, sem, *, alpha, n_tok):
    rows = x_ref.shape[0]
    base = pl.program_id(0) * rows

    def row_copy(k, r, src_row):
        return pltpu.make_async_copy(y_hbm.at[src_row], buf.at[k, r], sem)

    def issue(r, c):
        for k in range(TOP_K):
            row_copy(k, r, pos_ref[k * n_tok + base + r]).start()
        return c

    def drain(r, c):
        for k in range(TOP_K):
            row_copy(k, r, 0).wait()
        return c

    lax.fori_loop(0, rows, issue, 0, unroll=8)
    lax.fori_loop(0, rows, drain, 0, unroll=8)
    gates = gate_ref[...]
    y = gates[:, 0:1] * buf[0] + gates[:, 1:2] * buf[1]
    out = _ln_rows(alpha * x_ref[...] + y, g_ref[...], b_ref[...])
    o32_ref[...] = out
    o16_ref[...] = out.astype(BF16)


def _combine(pos, gates_t, x32, y_sorted, g, b, alpha):
    n, d = x32.shape
    tc = _tile(n, 256)
    row = pl.BlockSpec((tc, d), lambda i, p: (i, 0))
    vec = pl.BlockSpec((1, d), lambda i, p: (0, 0))
    return pl.pallas_call(
        functools.partial(_combine_kernel, alpha=alpha, n_tok=n),
        grid_spec=pltpu.PrefetchScalarGridSpec(
            num_scalar_prefetch=1, grid=(n // tc,),
            in_specs=[row, pl.BlockSpec((tc, TOP_K), lambda i, p: (i, 0)), vec, vec,
                      pl.BlockSpec(memory_space=pl.ANY)],
            out_specs=(row, row),
            scratch_shapes=[pltpu.VMEM((TOP_K, tc, d), F32), pltpu.SemaphoreType.DMA(())]),
        out_shape=(jax.ShapeDtypeStruct((n, d), F32), jax.ShapeDtypeStruct((n, d), BF16)),
        compiler_params=_params(("arbitrary",), 40), name="moe_combine",
    )(pos.reshape(-1), x32, gates_t, g.reshape(1, d), b.reshape(1, d), y_sorted)


def _moe_plan(idx, rank, counts):
    n = idx.shape[1]
    n_exp = counts.shape[0]
    n_tiles = (TOP_K * n) // MOE_TILE + n_exp
    subs_e = (counts + MOE_SUB - 1) // MOE_SUB
    tiles_e = (subs_e * MOE_SUB + MOE_TILE - 1) // MOE_TILE
    sub_base = subs_e // jnp.maximum(tiles_e, 1)
    sub_rem = subs_e - sub_base * tiles_e
    tile_end = jnp.cumsum(tiles_e)
    tile_start = tile_end - tiles_e

    def of_expert(table):
        return sum(jnp.where(idx == e, table[e], 0) for e in range(n_exp))

    base_t, rem_t = of_expert(sub_base), of_expert(sub_rem)
    sub_idx = rank // MOE_SUB
    big_subs = rem_t * (base_t + 1)
    tile_in_e = jnp.where(sub_idx < big_subs, sub_idx // (base_t + 1),
                          rem_t + (sub_idx - big_subs) // jnp.maximum(base_t, 1))
    first_sub = jnp.where(tile_in_e < rem_t, tile_in_e * (base_t + 1), big_subs + (tile_in_e - rem_t) * base_t)
    pos = (of_expert(tile_start) + tile_in_e) * MOE_TILE + (sub_idx - first_sub) * MOE_SUB + rank % MOE_SUB

    tile_id = jnp.arange(n_tiles, dtype=I32)
    used = tile_id < tile_end[-1]
    owner = jnp.minimum(jnp.sum((tile_id[:, None] >= tile_end[None, :]).astype(I32), axis=1), n_exp - 1)
    last_owner = jnp.max(jnp.where(used, owner, 0))
    tile_expert = jnp.where(used, owner, last_owner)
    holds_extra = (tile_id - tile_start[owner]) < sub_rem[owner]
    tile_nsub = jnp.where(used, sub_base[owner] + holds_extra.astype(I32), 0).astype(I32)
    return pos, tile_expert, tile_nsub


def _moe_layer(x32, w_router_t, w_gate, w_up, w_down, layer, ln_g, ln_b, alpha):
    n = x32.shape[0]
    idx, gates, rank, cnt = _route(x32, w_router_t, layer)
    pos, tile_expert, tile_nsub = _moe_plan(idx, rank, cnt[:, 0])
    tok = jnp.tile(jnp.arange(n, dtype=I32), TOP_K)
    src = jnp.zeros((tile_expert.shape[0] * MOE_TILE,), I32).at[pos.reshape(-1)].set(tok)

    y_sorted = _expert_ffn(x32, src, tile_expert, tile_nsub, w_gate, w_up, w_down, layer)
    return _combine(pos, gates.T, x32, y_sorted, ln_g, ln_b, alpha)


def kernel(x, positions, emb_ln_g, emb_ln_b, ln_g, ln_b, fourier_w_o, mla_w_in, mla_q_norm, mla_w_uq, mla_kv_norm, mla_w_ukv, mla_w_o, ffn_w_gate, ffn_w_up, ffn_w_down, moe_w_router, moe_w_gate, moe_w_up, moe_w_down):
    batch, seq, d = x.shape
    depth = ln_g.shape[0]
    alpha = (2.0 * depth) ** 0.25
    cos_t, sin_t = _rope_tables(positions)
    dft_tabs = _fourier_tables(seq)
    w_router_t = moe_w_router.transpose(0, 2, 1)

    x32, x16 = _layer_norm(x.reshape(batch * seq, d), emb_ln_g, emb_ln_b)
    for i in range(depth):
        j = i // 2
        if i % 2 == 0:
            mixed = _fourier_mixed(x16, dft_tabs, batch, seq)
            x32, x16 = _matmul_add_ln(mixed, fourier_w_o, j, x32, ln_g[i, 0], ln_b[i, 0], alpha, "fourier_out_ln")
            f = _dense_ffn(x16, ffn_w_gate, ffn_w_up, ffn_w_down, j)
            x32, x16 = _layer_norm(x32, ln_g[i, 1], ln_b[i, 1], f, alpha)
        else:
            attn = _mla_attention(x16, cos_t, sin_t, mla_w_in, mla_q_norm, mla_w_uq, mla_kv_norm, mla_w_ukv,
                                  j, batch, seq)
            x32, x16 = _matmul_add_ln(attn, mla_w_o, j, x32, ln_g[i, 0], ln_b[i, 0], alpha, "mla_out_ln")
            x32, x16 = _moe_layer(x32, w_router_t, moe_w_gate, moe_w_up, moe_w_down, j,
                                  ln_g[i, 1], ln_b[i, 1], alpha)
    return x32.reshape(batch, seq, d)
```

```python
import functools

import numpy as np
import jax
import jax.numpy as jnp
from jax import lax
from jax.experimental import pallas as pl
from jax.experimental.pallas import tpu as pltpu

F32, BF16, I32 = jnp.float32, jnp.bfloat16, jnp.int32

GROUP_DIM = 256
QK_NOPE = 128
QK_ROPE = 64
V_DIM = 128
HEAD_PAD = 256
TOP_K = 2
ROPE_THETA = 10000.0
LN_EPS = 1e-5
RMS_EPS = 1e-6
ATTN_SCALE = (QK_NOPE + QK_ROPE) ** -0.5
LOG2_E = 1.4426950408889634

LANES = 128
MIB = 1024 * 1024

MOE_TILE = 1024
MOE_SUB = 256


def _params(semantics, vmem_mib, **kw):
    return pltpu.CompilerParams(dimension_semantics=semantics, vmem_limit_bytes=vmem_mib * MIB, **kw)


def _tile(n, pref):
    t = min(n, pref)
    while n % t:
        t //= 2
    return t


def _resident(block_shape, index_map):
    return pl.BlockSpec(block_shape, index_map, pipeline_mode=pl.Buffered(1))


def _ln_rows(v, g, b):
    mu = jnp.mean(v, axis=-1, keepdims=True)
    vc = v - mu
    var = jnp.mean(vc * vc, axis=-1, keepdims=True)
    return vc * lax.rsqrt(var + LN_EPS) * g + b


def _ln_kernel(x_ref, g_ref, b_ref, o32_ref, o16_ref):
    y = _ln_rows(x_ref[...], g_ref[...], b_ref[...])
    o32_ref[...] = y
    o16_ref[...] = y.astype(BF16)


def _add_ln_kernel(x_ref, h_ref, g_ref, b_ref, o32_ref, o16_ref, *, alpha):
    y = _ln_rows(alpha * x_ref[...] + h_ref[...], g_ref[...], b_ref[...])
    o32_ref[...] = y
    o16_ref[...] = y.astype(BF16)


def _layer_norm(x, g, b, h=None, alpha=1.0):
    n, d = x.shape
    tm = _tile(n, 512)
    row = pl.BlockSpec((tm, d), lambda i: (i, 0))
    vec = pl.BlockSpec((1, d), lambda i: (0, 0))
    g2, b2 = g.reshape(1, d), b.reshape(1, d)
    out_shape = (jax.ShapeDtypeStruct((n, d), F32), jax.ShapeDtypeStruct((n, d), BF16))
    if h is None:
        return pl.pallas_call(
            _ln_kernel, grid=(n // tm,), in_specs=[row, vec, vec], out_specs=(row, row),
            out_shape=out_shape, compiler_params=_params(("parallel",), 40), name="ln",
        )(x, g2, b2)
    return pl.pallas_call(
        functools.partial(_add_ln_kernel, alpha=alpha), grid=(n // tm,),
        in_specs=[row, row, vec, vec], out_specs=(row, row),
        out_shape=out_shape, compiler_params=_params(("parallel",), 40), name="add_ln",
    )(x, h, g2, b2)


def _cast_weight_once(w_ref, w16_ref):
    @pl.when(pl.program_id(0) == 0)
    def _():
        w16_ref[...] = w_ref[...].astype(BF16)


def _mm_kernel(a_ref, w_ref, o_ref, w16_ref):
    _cast_weight_once(w_ref, w16_ref)
    o_ref[...] = jnp.dot(a_ref[...], w16_ref[...], preferred_element_type=F32).astype(o_ref.dtype)


def _matmul(a, w_stack, layer, out_dtype, tm_pref=512, name="mm"):
    m, k = a.shape
    n = w_stack.shape[2]
    tm = _tile(m, tm_pref)
    return pl.pallas_call(
        _mm_kernel, grid=(m // tm,),
        in_specs=[pl.BlockSpec((tm, k), lambda i: (i, 0)),
                  _resident((None, k, n), lambda i: (layer, 0, 0))],
        out_specs=pl.BlockSpec((tm, n), lambda i: (i, 0)),
        out_shape=jax.ShapeDtypeStruct((m, n), out_dtype),
        scratch_shapes=[pltpu.VMEM((k, n), BF16)],
        compiler_params=_params(("arbitrary",), 56), name=name,
    )(a, w_stack)


def _mm_add_ln_kernel(a_ref, w_ref, x_ref, g_ref, b_ref, o32_ref, o16_ref, w16_ref, *, alpha):
    _cast_weight_once(w_ref, w16_ref)
    h = jnp.dot(a_ref[...], w16_ref[...], preferred_element_type=F32)
    y = _ln_rows(alpha * x_ref[...] + h, g_ref[...], b_ref[...])
    o32_ref[...] = y
    o16_ref[...] = y.astype(BF16)


def _matmul_add_ln(a, w_stack, layer, x32, g, b, alpha, name):
    m, k = a.shape
    d = w_stack.shape[2]
    tm = _tile(m, 256)
    row = lambda w: pl.BlockSpec((tm, w), lambda i: (i, 0))
    vec = pl.BlockSpec((1, d), lambda i: (0, 0))
    return pl.pallas_call(
        functools.partial(_mm_add_ln_kernel, alpha=alpha), grid=(m // tm,),
        in_specs=[row(k), _resident((None, k, d), lambda i: (layer, 0, 0)), row(d), vec, vec],
        out_specs=(row(d), row(d)),
        out_shape=(jax.ShapeDtypeStruct((m, d), F32), jax.ShapeDtypeStruct((m, d), BF16)),
        scratch_shapes=[pltpu.VMEM((k, d), BF16)],
        compiler_params=_params(("arbitrary",), 56), name=name,
    )(a, w_stack, x32, g.reshape(1, d), b.reshape(1, d))


def _dft_tables(n, scale):
    n0 = 1 << (int(np.log2(n)) // 2)
    s = jnp.arange(n, dtype=I32)[None, :]
    step = np.float32(2.0 * np.pi / n)
    hi = ((jnp.arange(n // n0, dtype=I32)[:, None] * n0 * s) % n).astype(F32) * step
    lo = ((jnp.arange(n0, dtype=I32)[:, None] * s) % n).astype(F32) * step
    ch, sh = jnp.cos(hi)[:, None, :], jnp.sin(hi)[:, None, :]
    cl, sl = jnp.cos(lo)[None, :, :], jnp.sin(lo)[None, :, :]
    cos = (ch * cl - sh * sl).reshape(n, n)
    sin = (sh * cl + ch * sl).reshape(n, n)
    return cos * np.float32(scale), sin * np.float32(scale)


def _fourier_tables(seq):
    cg, sg = _dft_tables(GROUP_DIM, GROUP_DIM ** -0.5)
    cs, ss = _dft_tables(seq, seq ** -0.5)
    return (jnp.concatenate([cg, sg], axis=1).astype(BF16),
            jnp.concatenate([cs, -ss], axis=1).astype(BF16))


def _group_dft_kernel(x_ref, t_ref, z_ref, *, groups):
    t = t_ref[...]
    for g in range(groups):
        cols = slice(g * GROUP_DIM, (g + 1) * GROUP_DIM)
        z = jnp.dot(x_ref[:, cols], t, preferred_element_type=F32)
        z_ref[0, :, cols] = z[:, :GROUP_DIM].astype(BF16)
        z_ref[1, :, cols] = z[:, GROUP_DIM:].astype(BF16)


def _seq_dft_kernel(t_ref, z_ref, y_ref):
    y_ref[...] = jnp.dot(t_ref[...], z_ref[...], preferred_element_type=F32).astype(BF16)


def _fourier_mixed(x16, tabs, batch, seq):
    n, d = x16.shape
    t_group, t_seq = tabs
    tm = _tile(seq, 512)
    z = pl.pallas_call(
        functools.partial(_group_dft_kernel, groups=d // GROUP_DIM),
        grid=(batch, seq // tm),
        in_specs=[pl.BlockSpec((None, tm, d), lambda b, i: (b, i, 0)),
                  pl.BlockSpec((GROUP_DIM, 2 * GROUP_DIM), lambda b, i: (0, 0))],
        out_specs=pl.BlockSpec((None, 2, tm, d), lambda b, i: (b, 0, i, 0)),
        out_shape=jax.ShapeDtypeStruct((batch, 2, seq, d), BF16),
        compiler_params=_params(("parallel", "parallel"), 40), name="group_dft",
    )(x16.reshape(batch, seq, d), t_group)
    tr, tn = _tile(seq, 1024), _tile(d, 512)
    y = pl.pallas_call(
        _seq_dft_kernel, grid=(seq // tr, batch, d // tn),
        in_specs=[pl.BlockSpec((tr, 2 * seq), lambda r, b, j: (r, 0)),
                  pl.BlockSpec((None, 2 * seq, tn), lambda r, b, j: (b, 0, j))],
        out_specs=pl.BlockSpec((None, tr, tn), lambda r, b, j: (b, r, j)),
        out_shape=jax.ShapeDtypeStruct((batch, seq, d), BF16),
        compiler_params=_params(("parallel", "parallel", "parallel"), 40), name="seq_dft",
    )(t_seq, z.reshape(batch, 2 * seq, d))
    return y.reshape(n, d)


def _rope_tables(positions):
    inv_freq = ROPE_THETA ** (-jnp.arange(0, QK_ROPE, 2, dtype=F32) / QK_ROPE)
    ang = positions.astype(F32).reshape(-1, 1) * inv_freq
    cos, sin = jnp.cos(ang), jnp.sin(ang)
    zero = jnp.zeros((ang.shape[0], LANES - QK_ROPE), F32)
    return (jnp.concatenate([cos, cos, zero], axis=1), jnp.concatenate([-sin, sin, zero], axis=1))


def _rope_lanes(v, cos_t, sin_t):
    half = QK_ROPE // 2
    lane = lax.broadcasted_iota(I32, v.shape, 1)
    swapped = jnp.where(lane < half, pltpu.roll(v, LANES - half, 1), pltpu.roll(v, half, 1))
    return v * cos_t + swapped * sin_t


def _rms_rows(v, g):
    return v * lax.rsqrt(jnp.mean(v * v, axis=-1, keepdims=True) + RMS_EPS) * g


def _latent_kernel(x_ref, f_ref, g_ref, b_ref, w_ref, qg_ref, kvg_ref, cos_ref, sin_ref,
                   o32_ref, q_ref, kv_ref, kpe_ref, w16_ref, *, alpha, q_rank, kv_rank):
    _cast_weight_once(w_ref, w16_ref)
    x = _ln_rows(alpha * x_ref[...] + f_ref[...], g_ref[...], b_ref[...])
    o32_ref[...] = x
    lat = jnp.dot(x.astype(BF16), w16_ref[...], preferred_element_type=F32)
    q_ref[...] = _rms_rows(lat[:, :q_rank], qg_ref[...]).astype(BF16)
    kv_ref[...] = _rms_rows(lat[:, q_rank:q_rank + kv_rank], kvg_ref[...]).astype(BF16)
    kpe_ref[...] = _rope_lanes(_pad_rope(lat[:, q_rank + kv_rank:]), cos_ref[...], sin_ref[...]).astype(BF16)


def _pad_rope(v):
    return jnp.concatenate([v, jnp.zeros((v.shape[0], LANES - QK_ROPE), v.dtype)], axis=1)


def _q_up_kernel(a_ref, w_ref, cos_ref, sin_ref, o_ref, w16_ref, *, heads):
    _cast_weight_once(w_ref, w16_ref)
    acc = jnp.dot(a_ref[...], w16_ref[...], preferred_element_type=F32)
    cos_t, sin_t = cos_ref[...], sin_ref[...]
    scale = ATTN_SCALE * LOG2_E
    for h in range(heads):
        src, dst = h * (QK_NOPE + QK_ROPE), h * HEAD_PAD
        o_ref[:, dst:dst + QK_NOPE] = (acc[:, src:src + QK_NOPE] * scale).astype(BF16)
        rope = _rope_lanes(_pad_rope(acc[:, src + QK_NOPE:src + QK_NOPE + QK_ROPE]), cos_t, sin_t)
        o_ref[:, dst + QK_NOPE:dst + HEAD_PAD] = (rope * scale).astype(BF16)


def _attention_kernel(q_ref, kn_ref, kpe_ref, v_ref, o_ref, k_cat, v_ext, s_even, s_odd, m_even, m_odd,
                      *, key_chunk, q_tiles):
    t = pl.program_id(0)
    n_items = pl.num_programs(0) - 1
    seq = kn_ref.shape[0]

    @pl.when(t == 0)
    def _():
        s_odd[...] = jnp.zeros(s_odd.shape, F32)
        m_odd[...] = jnp.zeros(m_odd.shape, F32)
        v_ext[...] = jnp.ones(v_ext.shape, BF16)

    @pl.when((t % q_tiles == 0) & (t < n_items))
    def _():
        k_cat[:, :QK_NOPE] = kn_ref[...]
        k_cat[:, QK_NOPE:] = kpe_ref[...]

    @pl.when(((t - 1) % q_tiles == 0) & (t >= 1))
    def _():
        v_ext[:, :V_DIM] = v_ref[...]
        v_ext[:, V_DIM:] = jnp.ones((seq, HEAD_PAD - V_DIM), BF16)

    chunks = range(0, seq, key_chunk)

    def stages(s_cur, m_cur, s_prev, m_prev):
        q = q_ref[...]
        m_lanes = None
        for c in chunks:
            s = lax.dot_general(q, k_cat[c:c + key_chunk, :], (((1,), (1,)), ((), ())),
                                preferred_element_type=F32)
            s_cur[:, c:c + key_chunk] = s
            for l in range(0, key_chunk, LANES):
                m_lanes = s[:, l:l + LANES] if m_lanes is None else jnp.maximum(m_lanes, s[:, l:l + LANES])
        m_cur[...] = jnp.broadcast_to(jnp.max(m_lanes, axis=-1, keepdims=True), m_cur.shape)

        m_old = jnp.concatenate([m_prev[...]] * (key_chunk // LANES), axis=1)
        acc = None
        for c in chunks:
            p = jnp.exp2(s_prev[:, c:c + key_chunk] - m_old).astype(BF16)
            part = jnp.dot(p, v_ext[c:c + key_chunk, :], preferred_element_type=F32)
            acc = part if acc is None else acc + part
        o_ref[...] = (acc[:, :V_DIM] / acc[:, V_DIM:]).astype(BF16)

    @pl.when(t % 2 == 0)
    def _():
        stages(s_even, m_even, s_odd, m_odd)

    @pl.when(t % 2 == 1)
    def _():
        stages(s_odd, m_odd, s_even, m_even)


def _mla_attention(x32, f, ln_g, ln_b, alpha, cos_t, sin_t, w_in, q_norm, w_uq, kv_norm, w_ukv, layer, batch, seq):
    n, d = x32.shape
    q_rank, kv_rank = q_norm.shape[1], kv_norm.shape[1]
    heads = w_uq.shape[2] // (QK_NOPE + QK_ROPE)
    lat_w = q_rank + kv_rank + QK_ROPE

    tm = _tile(n, 512)
    row = lambda w: pl.BlockSpec((tm, w), lambda i: (i, 0))
    vec = pl.BlockSpec((1, d), lambda i: (0, 0))
    x_new, q_lat, kv_lat, k_pe = pl.pallas_call(
        functools.partial(_latent_kernel, alpha=alpha, q_rank=q_rank, kv_rank=kv_rank), grid=(n // tm,),
        in_specs=[row(d), row(d), vec, vec, _resident((None, d, lat_w), lambda i: (layer, 0, 0)),
                  pl.BlockSpec((None, 1, q_rank), lambda i: (layer, 0, 0)),
                  pl.BlockSpec((None, 1, kv_rank), lambda i: (layer, 0, 0)),
                  row(LANES), row(LANES)],
        out_specs=(row(d), row(q_rank), row(kv_rank), row(LANES)),
        out_shape=(jax.ShapeDtypeStruct((n, d), F32),
                   jax.ShapeDtypeStruct((n, q_rank), BF16), jax.ShapeDtypeStruct((n, kv_rank), BF16),
                   jax.ShapeDtypeStruct((n, LANES), BF16)),
        scratch_shapes=[pltpu.VMEM((d, lat_w), BF16)],
        compiler_params=_params(("arbitrary",), 56), name="mla_latent",
    )(x32, f, ln_g.reshape(1, d), ln_b.reshape(1, d), w_in, q_norm[:, None, :], kv_norm[:, None, :], cos_t, sin_t)

    q_cols = heads * (QK_NOPE + QK_ROPE)
    q_cat = pl.pallas_call(
        functools.partial(_q_up_kernel, heads=heads), grid=(n // tm,),
        in_specs=[row(q_rank), _resident((None, q_rank, q_cols), lambda i: (layer, 0, 0)),
                  row(LANES), row(LANES)],
        out_specs=row(heads * HEAD_PAD),
        out_shape=jax.ShapeDtypeStruct((n, heads * HEAD_PAD), BF16),
        scratch_shapes=[pltpu.VMEM((q_rank, q_cols), BF16)],
        compiler_params=_params(("arbitrary",), 56), name="mla_q_up",
    )(q_lat, w_uq, cos_t, sin_t)

    kv = _matmul(kv_lat, w_ukv, layer, BF16, name="mla_kv_up")

    tq = _tile(seq, 1024)
    q_tiles = seq // tq
    n_items = batch * heads * q_tiles

    def item(t):
        t = jnp.clip(t, 0, n_items - 1)
        return t // (heads * q_tiles), (t // q_tiles) % heads, t % q_tiles

    def score_q(t):
        b, h, i = item(t)
        return b, i, h

    def score_k(t):
        b, h, _ = item(t)
        return b, 0, 2 * h

    def value_v(t):
        b, h, _ = item(t - 1)
        return b, 0, 2 * h + 1

    def value_o(t):
        b, h, i = item(t - 1)
        return b, i, h

    kv3 = kv.reshape(batch, seq, -1)
    o = pl.pallas_call(
        functools.partial(_attention_kernel, key_chunk=_tile(seq, 512), q_tiles=q_tiles), grid=(n_items + 1,),
        in_specs=[pl.BlockSpec((None, tq, HEAD_PAD), score_q),
                  pl.BlockSpec((None, seq, QK_NOPE), score_k),
                  pl.BlockSpec((None, seq, LANES), lambda t: (item(t)[0], 0, 0)),
                  pl.BlockSpec((None, seq, V_DIM), value_v)],
        out_specs=pl.BlockSpec((None, tq, V_DIM), value_o),
        out_shape=jax.ShapeDtypeStruct((batch, seq, heads * V_DIM), BF16),
        scratch_shapes=[pltpu.VMEM((seq, HEAD_PAD), BF16), pltpu.VMEM((seq, HEAD_PAD), BF16),
                        pltpu.VMEM((tq, seq), F32), pltpu.VMEM((tq, seq), F32),
                        pltpu.VMEM((tq, LANES), F32), pltpu.VMEM((tq, LANES), F32)],
        compiler_params=_params(("arbitrary",), 56), name="mla_attention",
    )(q_cat.reshape(batch, seq, -1), kv3, k_pe.reshape(batch, seq, -1), kv3)
    return o.reshape(n, -1), x_new


def _swiglu_accumulate(x16, wg_ref, wu_ref, wd_ref, o_ref, rows):
    g = jnp.dot(x16, wg_ref[...].astype(BF16), preferred_element_type=F32)
    u = jnp.dot(x16, wu_ref[...].astype(BF16), preferred_element_type=F32)
    h = (g / (1.0 + jnp.exp(-g)) * u).astype(BF16)
    o_ref[0:rows, :] += jnp.dot(h, wd_ref[...].astype(BF16), preferred_element_type=F32)


def _dense_ffn_kernel(x_ref, wg_ref, wu_ref, wd_ref, o_ref):
    @pl.when(pl.program_id(1) == 0)
    def _():
        o_ref[...] = jnp.zeros_like(o_ref)

    _swiglu_accumulate(x_ref[...], wg_ref, wu_ref, wd_ref, o_ref, x_ref.shape[0])


def _dense_ffn(x16, w_gate, w_up, w_down, layer):
    n, d = x16.shape
    f = w_gate.shape[-1]
    tm, tf = _tile(n, 1024), _tile(f, 256)
    return pl.pallas_call(
        _dense_ffn_kernel, grid=(n // tm, f // tf),
        in_specs=[pl.BlockSpec((tm, d), lambda i, j: (i, 0)),
                  pl.BlockSpec((None, d, tf), lambda i, j: (layer, 0, j)),
                  pl.BlockSpec((None, d, tf), lambda i, j: (layer, 0, j)),
                  pl.BlockSpec((None, tf, d), lambda i, j: (layer, j, 0))],
        out_specs=pl.BlockSpec((tm, d), lambda i, j: (i, 0)),
        out_shape=jax.ShapeDtypeStruct((n, d), F32),
        compiler_params=_params(("parallel", "arbitrary"), 56), name="dense_ffn",
    )(x16, w_gate, w_up, w_down)


def _expert_ffn_kernel(te_ref, nsub_ref, src_ref, x_hbm, wg_ref, wu_ref, wd_ref, o_ref, xbuf, x16, sem, *, chunk):
    i, j = pl.program_id(0), pl.program_id(1)
    n_tiles = pl.num_programs(0)
    tm = o_ref.shape[0]

    def row_copy(r, src_row):
        return pltpu.make_async_copy(x_hbm.at[src_row], xbuf.at[r], sem)

    def issue_rows(tile, lo, count):
        def body(r, c):
            row_copy(lo + r, src_ref[tile * tm + lo + r]).start()
            return c
        lax.fori_loop(0, count, body, 0, unroll=8)

    def wait_rows(lo, count):
        pltpu.make_async_copy(x_hbm.at[pl.ds(0, count)], xbuf.at[pl.ds(lo, count)], sem).wait()

    def for_each_sub(n_sub, fn):
        def body(s, c):
            fn(s * MOE_SUB)
            return c
        lax.fori_loop(0, n_sub, body, 0)

    @pl.when(j == 0)
    def _():
        @pl.when(i == 0)
        def _():
            for_each_sub(nsub_ref[0], lambda lo: issue_rows(0, lo, MOE_SUB))
        for_each_sub(nsub_ref[i], lambda lo: wait_rows(lo, MOE_SUB))
        o_ref[...] = jnp.zeros_like(o_ref)
        for n in range(1, tm // MOE_SUB + 1):
            @pl.when(nsub_ref[i] == n)
            def _():
                x16[0:n * MOE_SUB, :] = xbuf[0:n * MOE_SUB, :].astype(BF16)

    nxt = jnp.minimum(i + 1, n_tiles - 1)
    lo = (j - 1) * chunk

    @pl.when((j >= 1) & (i + 1 < n_tiles) & (lo < nsub_ref[nxt] * MOE_SUB))
    def _():
        issue_rows(nxt, lo, chunk)

    for n in range(1, tm // MOE_SUB + 1):
        @pl.when(nsub_ref[i] == n)
        def _():
            rows = n * MOE_SUB
            _swiglu_accumulate(x16[0:rows, :], wg_ref, wu_ref, wd_ref, o_ref, rows)


def _expert_ffn(x32, src, tile_expert, tile_nsub, w_gate, w_up, w_down, layer):
    n_tiles = tile_expert.shape[0]
    d = x32.shape[1]
    f = w_gate.shape[-1]
    tm, tf = MOE_TILE, _tile(f, 256)
    nf = f // tf
    chunk = tm // (1 << int(np.log2(nf - 1)))
    assert MOE_SUB % chunk == 0, "too few hidden-column steps to stream in the next tile"

    def w_col(i, j, te, ns, src):
        return (layer, te[i], 0, jnp.where(ns[i] > 0, j, nf - 1))

    def w_row(i, j, te, ns, src):
        return (layer, te[i], jnp.where(ns[i] > 0, j, nf - 1), 0)

    return pl.pallas_call(
        functools.partial(_expert_ffn_kernel, chunk=chunk),
        grid_spec=pltpu.PrefetchScalarGridSpec(
            num_scalar_prefetch=3, grid=(n_tiles, nf),
            in_specs=[pl.BlockSpec(memory_space=pl.ANY),
                      pl.BlockSpec((None, None, d, tf), w_col), pl.BlockSpec((None, None, d, tf), w_col),
                      pl.BlockSpec((None, None, tf, d), w_row)],
            out_specs=pl.BlockSpec((tm, d), lambda i, j, te, ns, src: (i, 0)),
            scratch_shapes=[pltpu.VMEM((tm, d), F32), pltpu.VMEM((tm, d), BF16),
                            pltpu.SemaphoreType.DMA(())]),
        out_shape=jax.ShapeDtypeStruct((n_tiles * tm, d), F32),
        compiler_params=_params(("arbitrary", "arbitrary"), 56), name="expert_ffn",
    )(tile_expert, tile_nsub, src, x32, w_gate, w_up, w_down)


def _router_kernel(x_ref, w_ref, idx_ref, gate_ref, rank_ref, cnt_ref, carry_ref):
    tm = x_ref.shape[0]
    n_exp = w_ref.shape[0]

    @pl.when(pl.program_id(0) == 0)
    def _():
        carry_ref[...] = jnp.zeros_like(carry_ref)

    x = x_ref[...]
    xh = x.astype(BF16)
    xl = (x - xh.astype(F32)).astype(BF16)
    w = w_ref[...]
    wh = w.astype(BF16)
    wl = (w - wh.astype(F32)).astype(BF16)
    nt_dims = (((1,), (1,)), ((), ()))
    logits = (lax.dot_general(wh, xh, nt_dims, preferred_element_type=F32)
              + lax.dot_general(wh, xl, nt_dims, preferred_element_type=F32)
              + lax.dot_general(wl, xh, nt_dims, preferred_element_type=F32))

    e_id = lax.broadcasted_iota(I32, (n_exp, tm), 0).astype(F32)
    m1 = jnp.max(logits, axis=0, keepdims=True)
    i1 = jnp.min(jnp.where(logits == m1, e_id, float(n_exp)), axis=0, keepdims=True)
    first = e_id == i1
    rest = jnp.where(first, -jnp.inf, logits)
    m2 = jnp.max(rest, axis=0, keepdims=True)
    i2 = jnp.min(jnp.where(rest == m2, e_id, float(n_exp)), axis=0, keepdims=True)
    second = e_id == i2
    d = jnp.exp(m2 - m1)
    g1 = 1.0 / (1.0 + d)

    chosen = jnp.where(first | second, 1.0, 0.0)
    before = (lax.broadcasted_iota(I32, (tm, tm), 0) < lax.broadcasted_iota(I32, (tm, tm), 1))
    rank = jnp.dot(chosen.astype(BF16), jnp.where(before, 1.0, 0.0).astype(BF16),
                   preferred_element_type=F32) + carry_ref[:, :1]
    r1 = jnp.sum(jnp.where(first, rank, 0.0), axis=0, keepdims=True)
    r2 = jnp.sum(jnp.where(second, rank, 0.0), axis=0, keepdims=True)
    carry_ref[...] = carry_ref[...] + jnp.sum(chosen, axis=1, keepdims=True)

    idx_ref[0:1, :] = i1.astype(I32)
    idx_ref[1:2, :] = i2.astype(I32)
    gate_ref[0:1, :] = g1
    gate_ref[1:2, :] = d * g1
    rank_ref[0:1, :] = r1.astype(I32)
    rank_ref[1:2, :] = r2.astype(I32)
    cnt_ref[...] = carry_ref[...].astype(I32)


def _route(x32, w_router_t, layer):
    n, d = x32.shape
    n_exp = w_router_t.shape[1]
    tm = _tile(n, 1024)
    slot = pl.BlockSpec((TOP_K, tm), lambda i: (0, i))
    return pl.pallas_call(
        _router_kernel, grid=(n // tm,),
        in_specs=[pl.BlockSpec((tm, d), lambda i: (i, 0)),
                  pl.BlockSpec((None, n_exp, d), lambda i: (layer, 0, 0))],
        out_specs=(slot, slot, slot, pl.BlockSpec((n_exp, LANES), lambda i: (0, 0))),
        out_shape=(jax.ShapeDtypeStruct((TOP_K, n), I32), jax.ShapeDtypeStruct((TOP_K, n), F32),
                   jax.ShapeDtypeStruct((TOP_K, n), I32), jax.ShapeDtypeStruct((n_exp, LANES), I32)),
        scratch_shapes=[pltpu.VMEM((n_exp, LANES), F32)],
        compiler_params=_params(("arbitrary",), 40), name="router",
    )(x32, w_router_t)


def _combine_kernel(pos_ref, x_ref, gate_ref, g_ref, b_ref, y_hbm, o32_ref, o16_ref, buf_even, buf_odd, sems,
                    *, alpha, n_tok):
    i = pl.program_id(0)
    rows = x_ref.shape[0]

    def row_copy(buf, sem, k, r, src_row):
        return pltpu.make_async_copy(y_hbm.at[src_row], buf.at[k, r], sem)

    def issue(step, buf, sem):
        def body(r, c):
            for k in range(TOP_K):
                row_copy(buf, sem, k, r, pos_ref[k * n_tok + step * rows + r]).start()
            return c
        lax.fori_loop(0, rows, body, 0, unroll=8)

    def drain(buf, sem):
        for k in range(TOP_K):
            pltpu.make_async_copy(y_hbm.at[pl.ds(0, rows)], buf.at[k], sem).wait()

    def finish(buf):
        gates = gate_ref[...]
        y = gates[:, 0:1] * buf[0] + gates[:, 1:2] * buf[1]
        out = _ln_rows(alpha * x_ref[...] + y, g_ref[...], b_ref[...])
        o32_ref[...] = out
        o16_ref[...] = out.astype(BF16)

    def step(buf, sem, next_buf, next_sem):
        @pl.when(i + 1 < pl.num_programs(0))
        def _():
            issue(i + 1, next_buf, next_sem)
        drain(buf, sem)
        finish(buf)

    @pl.when(i == 0)
    def _():
        issue(0, buf_even, sems.at[0])

    @pl.when(i % 2 == 0)
    def _():
        step(buf_even, sems.at[0], buf_odd, sems.at[1])

    @pl.when(i % 2 == 1)
    def _():
        step(buf_odd, sems.at[1], buf_even, sems.at[0])


def _combine(pos, gates_t, x32, y_sorted, g, b, alpha):
    n, d = x32.shape
    tc = _tile(n, 256)
    row = pl.BlockSpec((tc, d), lambda i, p: (i, 0))
    vec = pl.BlockSpec((1, d), lambda i, p: (0, 0))
    return pl.pallas_call(
        functools.partial(_combine_kernel, alpha=alpha, n_tok=n),
        grid_spec=pltpu.PrefetchScalarGridSpec(
            num_scalar_prefetch=1, grid=(n // tc,),
            in_specs=[row, pl.BlockSpec((tc, TOP_K), lambda i, p: (i, 0)), vec, vec,
                      pl.BlockSpec(memory_space=pl.ANY)],
            out_specs=(row, row),
            scratch_shapes=[pltpu.VMEM((TOP_K, tc, d), F32), pltpu.VMEM((TOP_K, tc, d), F32),
                            pltpu.SemaphoreType.DMA((2,))]),
        out_shape=(jax.ShapeDtypeStruct((n, d), F32), jax.ShapeDtypeStruct((n, d), BF16)),
        compiler_params=_params(("arbitrary",), 40), name="moe_combine",
    )(pos.reshape(-1), x32, gates_t, g.reshape(1, d), b.reshape(1, d), y_sorted)


def _moe_plan(idx, rank, counts):
    n = idx.shape[1]
    n_exp = counts.shape[0]
    n_tiles = (TOP_K * n) // MOE_TILE + n_exp
    subs_e = (counts + MOE_SUB - 1) // MOE_SUB
    tiles_e = (subs_e * MOE_SUB + MOE_TILE - 1) // MOE_TILE
    sub_base = subs_e // jnp.maximum(tiles_e, 1)
    sub_rem = subs_e - sub_base * tiles_e
    tile_end = jnp.cumsum(tiles_e)
    tile_start = tile_end - tiles_e

    def of_expert(table):
        return sum(jnp.where(idx == e, table[e], 0) for e in range(n_exp))

    base_t, rem_t = of_expert(sub_base), of_expert(sub_rem)
    sub_idx = rank // MOE_SUB
    big_subs = rem_t * (base_t + 1)
    tile_in_e = jnp.where(sub_idx < big_subs, sub_idx // (base_t + 1),
                          rem_t + (sub_idx - big_subs) // jnp.maximum(base_t, 1))
    first_sub = jnp.where(tile_in_e < rem_t, tile_in_e * (base_t + 1), big_subs + (tile_in_e - rem_t) * base_t)
    pos = (of_expert(tile_start) + tile_in_e) * MOE_TILE + (sub_idx - first_sub) * MOE_SUB + rank % MOE_SUB

    tile_id = jnp.arange(n_tiles, dtype=I32)
    used = tile_id < tile_end[-1]
    owner = jnp.minimum(jnp.sum((tile_id[:, None] >= tile_end[None, :]).astype(I32), axis=1), n_exp - 1)
    last_owner = jnp.max(jnp.where(used, owner, 0))
    tile_expert = jnp.where(used, owner, last_owner)
    holds_extra = (tile_id - tile_start[owner]) < sub_rem[owner]
    tile_nsub = jnp.where(used, sub_base[owner] + holds_extra.astype(I32), 0).astype(I32)
    return pos, tile_expert, tile_nsub


def _moe_layer(x32, w_router_t, w_gate, w_up, w_down, layer, ln_g, ln_b, alpha):
    n = x32.shape[0]
    idx, gates, rank, cnt = _route(x32, w_router_t, layer)
    pos, tile_expert, tile_nsub = _moe_plan(idx, rank, cnt[:, 0])
    tok = jnp.tile(jnp.arange(n, dtype=I32), TOP_K)
    src = jnp.zeros((tile_expert.shape[0] * MOE_TILE,), I32).at[pos.reshape(-1)].set(
        tok, unique_indices=True, mode="promise_in_bounds")

    y_sorted = _expert_ffn(x32, src, tile_expert, tile_nsub, w_gate, w_up, w_down, layer)
    return _combine(pos, gates.T, x32, y_sorted, ln_g, ln_b, alpha)


def kernel(x, positions, emb_ln_g, emb_ln_b, ln_g, ln_b, fourier_w_o, mla_w_in, mla_q_norm, mla_w_uq, mla_kv_norm, mla_w_ukv, mla_w_o, ffn_w_gate, ffn_w_up, ffn_w_down, moe_w_router, moe_w_gate, moe_w_up, moe_w_down):
    batch, seq, d = x.shape
    depth = ln_g.shape[0]
    alpha = (2.0 * depth) ** 0.25
    cos_t, sin_t = _rope_tables(positions)
    dft_tabs = _fourier_tables(seq)
    w_router_t = moe_w_router.transpose(0, 2, 1)

    x32, x16 = _layer_norm(x.reshape(batch * seq, d), emb_ln_g, emb_ln_b)
    ffn_out = None
    for i in range(depth):
        j = i // 2
        if i % 2 == 0:
            mixed = _fourier_mixed(x16, dft_tabs, batch, seq)
            x32, x16 = _matmul_add_ln(mixed, fourier_w_o, j, x32, ln_g[i, 0], ln_b[i, 0], alpha, "fourier_out_ln")
            ffn_out = _dense_ffn(x16, ffn_w_gate, ffn_w_up, ffn_w_down, j)
        else:
            attn, x32 = _mla_attention(x32, ffn_out, ln_g[i - 1, 1], ln_b[i - 1, 1], alpha, cos_t, sin_t,
                                       mla_w_in, mla_q_norm, mla_w_uq, mla_kv_norm, mla_w_ukv, j, batch, seq)
            ffn_out = None
            x32, x16 = _matmul_add_ln(attn, mla_w_o, j, x32, ln_g[i, 0], ln_b[i, 0], alpha, "mla_out_ln")
            x32, x16 = _moe_layer(x32, w_router_t, moe_w_gate, moe_w_up, moe_w_down, j,
                                  ln_g[i, 1], ln_b[i, 1], alpha)
    if ffn_out is not None:
        x32, _ = _layer_norm(x32, ln_g[depth - 1, 1], ln_b[depth - 1, 1], ffn_out, alpha)
    return x32.reshape(batch, seq, d)
```

```python
import functools

import numpy as np
import jax
import jax.numpy as jnp
from jax import lax
from jax.experimental import pallas as pl
from jax.experimental.pallas import tpu as pltpu

F32, BF16, I32 = jnp.float32, jnp.bfloat16, jnp.int32

GROUP_DIM = 256
QK_NOPE = 128
QK_ROPE = 64
V_DIM = 128
HEAD_PAD = 256
TOP_K = 2
ROPE_THETA = 10000.0
LN_EPS = 1e-5
RMS_EPS = 1e-6
ATTN_SCALE = (QK_NOPE + QK_ROPE) ** -0.5
LOG2_E = 1.4426950408889634

LANES = 128
MIB = 1024 * 1024

MOE_TILE = 1024
MOE_SUB = 256


def _params(semantics, vmem_mib, **kw):
    return pltpu.CompilerParams(dimension_semantics=semantics, vmem_limit_bytes=vmem_mib * MIB, **kw)


def _tile(n, pref):
    t = min(n, pref)
    while n % t:
        t //= 2
    return t


def _resident(block_shape, index_map):
    return pl.BlockSpec(block_shape, index_map, pipeline_mode=pl.Buffered(1))


def _ln_rows(v, g, b):
    mu = jnp.mean(v, axis=-1, keepdims=True)
    vc = v - mu
    var = jnp.mean(vc * vc, axis=-1, keepdims=True)
    return vc * lax.rsqrt(var + LN_EPS) * g + b


def _ln_kernel(x_ref, g_ref, b_ref, o32_ref, o16_ref):
    y = _ln_rows(x_ref[...], g_ref[...], b_ref[...])
    o32_ref[...] = y
    o16_ref[...] = y.astype(BF16)


def _add_ln_kernel(x_ref, h_ref, g_ref, b_ref, o32_ref, o16_ref, *, alpha):
    y = _ln_rows(alpha * x_ref[...] + h_ref[...], g_ref[...], b_ref[...])
    o32_ref[...] = y
    o16_ref[...] = y.astype(BF16)


def _layer_norm(x, g, b, h=None, alpha=1.0):
    n, d = x.shape
    tm = _tile(n, 512)
    row = pl.BlockSpec((tm, d), lambda i: (i, 0))
    vec = pl.BlockSpec((1, d), lambda i: (0, 0))
    g2, b2 = g.reshape(1, d), b.reshape(1, d)
    out_shape = (jax.ShapeDtypeStruct((n, d), F32), jax.ShapeDtypeStruct((n, d), BF16))
    if h is None:
        return pl.pallas_call(
            _ln_kernel, grid=(n // tm,), in_specs=[row, vec, vec], out_specs=(row, row),
            out_shape=out_shape, compiler_params=_params(("parallel",), 40), name="ln",
        )(x, g2, b2)
    return pl.pallas_call(
        functools.partial(_add_ln_kernel, alpha=alpha), grid=(n // tm,),
        in_specs=[row, row, vec, vec], out_specs=(row, row),
        out_shape=out_shape, compiler_params=_params(("parallel",), 40), name="add_ln",
    )(x, h, g2, b2)


def _cast_weight_once(w_ref, w16_ref):
    @pl.when(pl.program_id(0) == 0)
    def _():
        w16_ref[...] = w_ref[...].astype(BF16)


def _mm_kernel(a_ref, w_ref, o_ref, w16_ref):
    _cast_weight_once(w_ref, w16_ref)
    o_ref[...] = jnp.dot(a_ref[...], w16_ref[...], preferred_element_type=F32).astype(o_ref.dtype)


def _matmul(a, w_stack, layer, out_dtype, tm_pref=512, name="mm"):
    m, k = a.shape
    n = w_stack.shape[2]
    tm = _tile(m, tm_pref)
    return pl.pallas_call(
        _mm_kernel, grid=(m // tm,),
        in_specs=[pl.BlockSpec((tm, k), lambda i: (i, 0)),
                  _resident((None, k, n), lambda i: (layer, 0, 0))],
        out_specs=pl.BlockSpec((tm, n), lambda i: (i, 0)),
        out_shape=jax.ShapeDtypeStruct((m, n), out_dtype),
        scratch_shapes=[pltpu.VMEM((k, n), BF16)],
        compiler_params=_params(("arbitrary",), 56), name=name,
    )(a, w_stack)


def _mm_add_ln_kernel(a_ref, w_ref, x_ref, g_ref, b_ref, o32_ref, o16_ref, w16_ref, *, alpha):
    _cast_weight_once(w_ref, w16_ref)
    h = jnp.dot(a_ref[...], w16_ref[...], preferred_element_type=F32)
    y = _ln_rows(alpha * x_ref[...] + h, g_ref[...], b_ref[...])
    o32_ref[...] = y
    o16_ref[...] = y.astype(BF16)


def _matmul_add_ln(a, w_stack, layer, x32, g, b, alpha, name):
    m, k = a.shape
    d = w_stack.shape[2]
    tm = _tile(m, 256)
    row = lambda w: pl.BlockSpec((tm, w), lambda i: (i, 0))
    vec = pl.BlockSpec((1, d), lambda i: (0, 0))
    return pl.pallas_call(
        functools.partial(_mm_add_ln_kernel, alpha=alpha), grid=(m // tm,),
        in_specs=[row(k), _resident((None, k, d), lambda i: (layer, 0, 0)), row(d), vec, vec],
        out_specs=(row(d), row(d)),
        out_shape=(jax.ShapeDtypeStruct((m, d), F32), jax.ShapeDtypeStruct((m, d), BF16)),
        scratch_shapes=[pltpu.VMEM((k, d), BF16)],
        compiler_params=_params(("arbitrary",), 56), name=name,
    )(a, w_stack, x32, g.reshape(1, d), b.reshape(1, d))


def _dft_tables(n, scale):
    n0 = 1 << (int(np.log2(n)) // 2)
    s = jnp.arange(n, dtype=I32)[None, :]
    step = np.float32(2.0 * np.pi / n)
    hi = ((jnp.arange(n // n0, dtype=I32)[:, None] * n0 * s) % n).astype(F32) * step
    lo = ((jnp.arange(n0, dtype=I32)[:, None] * s) % n).astype(F32) * step
    ch, sh = jnp.cos(hi)[:, None, :], jnp.sin(hi)[:, None, :]
    cl, sl = jnp.cos(lo)[None, :, :], jnp.sin(lo)[None, :, :]
    cos = (ch * cl - sh * sl).reshape(n, n)
    sin = (sh * cl + ch * sl).reshape(n, n)
    return cos * np.float32(scale), sin * np.float32(scale)


def _fourier_tables(seq):
    cg, sg = _dft_tables(GROUP_DIM, GROUP_DIM ** -0.5)
    cs, ss = _dft_tables(seq, seq ** -0.5)
    return (jnp.concatenate([cg, sg], axis=1).astype(BF16),
            jnp.concatenate([cs, -ss], axis=1).astype(BF16))


def _group_dft_kernel(x_ref, t_ref, z_ref, *, groups):
    t = t_ref[...]
    for g in range(groups):
        cols = slice(g * GROUP_DIM, (g + 1) * GROUP_DIM)
        z = jnp.dot(x_ref[:, cols], t, preferred_element_type=F32)
        z_ref[0, :, cols] = z[:, :GROUP_DIM].astype(BF16)
        z_ref[1, :, cols] = z[:, GROUP_DIM:].astype(BF16)


def _seq_dft_kernel(t_ref, z_ref, y_ref):
    y_ref[...] = jnp.dot(t_ref[...], z_ref[...], preferred_element_type=F32).astype(BF16)


def _fourier_mixed(x16, tabs, batch, seq):
    n, d = x16.shape
    t_group, t_seq = tabs
    tm = _tile(seq, 512)
    z = pl.pallas_call(
        functools.partial(_group_dft_kernel, groups=d // GROUP_DIM),
        grid=(batch, seq // tm),
        in_specs=[pl.BlockSpec((None, tm, d), lambda b, i: (b, i, 0)),
                  pl.BlockSpec((GROUP_DIM, 2 * GROUP_DIM), lambda b, i: (0, 0))],
        out_specs=pl.BlockSpec((None, 2, tm, d), lambda b, i: (b, 0, i, 0)),
        out_shape=jax.ShapeDtypeStruct((batch, 2, seq, d), BF16),
        compiler_params=_params(("parallel", "parallel"), 40), name="group_dft",
    )(x16.reshape(batch, seq, d), t_group)
    tr, tn = _tile(seq, 1024), _tile(d, 512)
    y = pl.pallas_call(
        _seq_dft_kernel, grid=(seq // tr, batch, d // tn),
        in_specs=[pl.BlockSpec((tr, 2 * seq), lambda r, b, j: (r, 0)),
                  pl.BlockSpec((None, 2 * seq, tn), lambda r, b, j: (b, 0, j))],
        out_specs=pl.BlockSpec((None, tr, tn), lambda r, b, j: (b, r, j)),
        out_shape=jax.ShapeDtypeStruct((batch, seq, d), BF16),
        compiler_params=_params(("parallel", "parallel", "parallel"), 40), name="seq_dft",
    )(t_seq, z.reshape(batch, 2 * seq, d))
    return y.reshape(n, d)


def _rope_tables(positions):
    inv_freq = ROPE_THETA ** (-jnp.arange(0, QK_ROPE, 2, dtype=F32) / QK_ROPE)
    ang = positions.astype(F32).reshape(-1, 1) * inv_freq
    cos, sin = jnp.cos(ang), jnp.sin(ang)
    zero = jnp.zeros((ang.shape[0], LANES - QK_ROPE), F32)
    return (jnp.concatenate([cos, cos, zero], axis=1), jnp.concatenate([-sin, sin, zero], axis=1))


def _rope_lanes(v, cos_t, sin_t):
    half = QK_ROPE // 2
    lane = lax.broadcasted_iota(I32, v.shape, 1)
    swapped = jnp.where(lane < half, pltpu.roll(v, LANES - half, 1), pltpu.roll(v, half, 1))
    return v * cos_t + swapped * sin_t


def _rms_rows(v, g):
    return v * lax.rsqrt(jnp.mean(v * v, axis=-1, keepdims=True) + RMS_EPS) * g


def _latent_kernel(x_ref, f_ref, g_ref, b_ref, w_ref, qg_ref, kvg_ref, cos_ref, sin_ref,
                   o32_ref, q_ref, kv_ref, kpe_ref, w16_ref, *, alpha, q_rank, kv_rank):
    _cast_weight_once(w_ref, w16_ref)
    x = _ln_rows(alpha * x_ref[...] + f_ref[...], g_ref[...], b_ref[...])
    o32_ref[...] = x
    lat = jnp.dot(x.astype(BF16), w16_ref[...], preferred_element_type=F32)
    q_ref[...] = _rms_rows(lat[:, :q_rank], qg_ref[...]).astype(BF16)
    kv_ref[...] = _rms_rows(lat[:, q_rank:q_rank + kv_rank], kvg_ref[...]).astype(BF16)
    kpe_ref[...] = _rope_lanes(_pad_rope(lat[:, q_rank + kv_rank:]), cos_ref[...], sin_ref[...]).astype(BF16)


def _pad_rope(v):
    return jnp.concatenate([v, jnp.zeros((v.shape[0], LANES - QK_ROPE), v.dtype)], axis=1)


def _q_up_kernel(a_ref, w_ref, cos_ref, sin_ref, o_ref, w16_ref, *, heads):
    _cast_weight_once(w_ref, w16_ref)
    acc = jnp.dot(a_ref[...], w16_ref[...], preferred_element_type=F32)
    cos_t, sin_t = cos_ref[...], sin_ref[...]
    scale = ATTN_SCALE * LOG2_E
    for h in range(heads):
        src, dst = h * (QK_NOPE + QK_ROPE), h * HEAD_PAD
        o_ref[:, dst:dst + QK_NOPE] = (acc[:, src:src + QK_NOPE] * scale).astype(BF16)
        rope = _rope_lanes(_pad_rope(acc[:, src + QK_NOPE:src + QK_NOPE + QK_ROPE]), cos_t, sin_t)
        o_ref[:, dst + QK_NOPE:dst + HEAD_PAD] = (rope * scale).astype(BF16)


def _attention_kernel(q_ref, kn_ref, kpe_ref, v_ref, o_ref, k_cat, v_ext, s_even, s_odd, m_even, m_odd,
                      *, key_chunk, q_tiles):
    t = pl.program_id(0)
    n_items = pl.num_programs(0) - 1
    seq = kn_ref.shape[0]

    @pl.when(t == 0)
    def _():
        s_odd[...] = jnp.zeros(s_odd.shape, F32)
        m_odd[...] = jnp.zeros(m_odd.shape, F32)
        v_ext[...] = jnp.ones(v_ext.shape, BF16)

    @pl.when((t % q_tiles == 0) & (t < n_items))
    def _():
        k_cat[:, :QK_NOPE] = kn_ref[...]
        k_cat[:, QK_NOPE:] = kpe_ref[...]

    @pl.when(((t - 1) % q_tiles == 0) & (t >= 1))
    def _():
        v_ext[:, :V_DIM] = v_ref[...]
        v_ext[:, V_DIM:] = jnp.ones((seq, HEAD_PAD - V_DIM), BF16)

    chunks = range(0, seq, key_chunk)

    def stages(s_cur, m_cur, s_prev, m_prev):
        q = q_ref[...]
        m_lanes = None
        for c in chunks:
            s = lax.dot_general(q, k_cat[c:c + key_chunk, :], (((1,), (1,)), ((), ())),
                                preferred_element_type=F32)
            s_cur[:, c:c + key_chunk] = s
            for l in range(0, key_chunk, LANES):
                m_lanes = s[:, l:l + LANES] if m_lanes is None else jnp.maximum(m_lanes, s[:, l:l + LANES])
        m_cur[...] = jnp.broadcast_to(jnp.max(m_lanes, axis=-1, keepdims=True), m_cur.shape)

        m_old = jnp.concatenate([m_prev[...]] * (key_chunk // LANES), axis=1)
        acc = None
        for c in chunks:
            p = jnp.exp2(s_prev[:, c:c + key_chunk] - m_old).astype(BF16)
            part = jnp.dot(p, v_ext[c:c + key_chunk, :], preferred_element_type=F32)
            acc = part if acc is None else acc + part
        o_ref[...] = (acc[:, :V_DIM] / acc[:, V_DIM:]).astype(BF16)

    @pl.when(t % 2 == 0)
    def _():
        stages(s_even, m_even, s_odd, m_odd)

    @pl.when(t % 2 == 1)
    def _():
        stages(s_odd, m_odd, s_even, m_even)


def _mla_attention(x32, f, ln_g, ln_b, alpha, cos_t, sin_t, w_in, q_norm, w_uq, kv_norm, w_ukv, layer, batch, seq):
    n, d = x32.shape
    q_rank, kv_rank = q_norm.shape[1], kv_norm.shape[1]
    heads = w_uq.shape[2] // (QK_NOPE + QK_ROPE)
    lat_w = q_rank + kv_rank + QK_ROPE

    tm = _tile(n, 512)
    row = lambda w: pl.BlockSpec((tm, w), lambda i: (i, 0))
    vec = pl.BlockSpec((1, d), lambda i: (0, 0))
    x_new, q_lat, kv_lat, k_pe = pl.pallas_call(
        functools.partial(_latent_kernel, alpha=alpha, q_rank=q_rank, kv_rank=kv_rank), grid=(n // tm,),
        in_specs=[row(d), row(d), vec, vec, _resident((None, d, lat_w), lambda i: (layer, 0, 0)),
                  pl.BlockSpec((None, 1, q_rank), lambda i: (layer, 0, 0)),
                  pl.BlockSpec((None, 1, kv_rank), lambda i: (layer, 0, 0)),
                  row(LANES), row(LANES)],
        out_specs=(row(d), row(q_rank), row(kv_rank), row(LANES)),
        out_shape=(jax.ShapeDtypeStruct((n, d), F32),
                   jax.ShapeDtypeStruct((n, q_rank), BF16), jax.ShapeDtypeStruct((n, kv_rank), BF16),
                   jax.ShapeDtypeStruct((n, LANES), BF16)),
        scratch_shapes=[pltpu.VMEM((d, lat_w), BF16)],
        compiler_params=_params(("arbitrary",), 56), name="mla_latent",
    )(x32, f, ln_g.reshape(1, d), ln_b.reshape(1, d), w_in, q_norm[:, None, :], kv_norm[:, None, :], cos_t, sin_t)

    q_cols = heads * (QK_NOPE + QK_ROPE)
    q_cat = pl.pallas_call(
        functools.partial(_q_up_kernel, heads=heads), grid=(n // tm,),
        in_specs=[row(q_rank), _resident((None, q_rank, q_cols), lambda i: (layer, 0, 0)),
                  row(LANES), row(LANES)],
        out_specs=row(heads * HEAD_PAD),
        out_shape=jax.ShapeDtypeStruct((n, heads * HEAD_PAD), BF16),
        scratch_shapes=[pltpu.VMEM((q_rank, q_cols), BF16)],
        compiler_params=_params(("arbitrary",), 56), name="mla_q_up",
    )(q_lat, w_uq, cos_t, sin_t)

    kv = _matmul(kv_lat, w_ukv, layer, BF16, name="mla_kv_up")

    tq = _tile(seq, 1024)
    q_tiles = seq // tq
    n_items = batch * heads * q_tiles

    def item(t):
        t = jnp.clip(t, 0, n_items - 1)
        return t // (heads * q_tiles), (t // q_tiles) % heads, t % q_tiles

    def score_q(t):
        b, h, i = item(t)
        return b, i, h

    def score_k(t):
        b, h, _ = item(t)
        return b, 0, 2 * h

    def value_v(t):
        b, h, _ = item(t - 1)
        return b, 0, 2 * h + 1

    def value_o(t):
        b, h, i = item(t - 1)
        return b, i, h

    kv3 = kv.reshape(batch, seq, -1)
    o = pl.pallas_call(
        functools.partial(_attention_kernel, key_chunk=_tile(seq, 512), q_tiles=q_tiles), grid=(n_items + 1,),
        in_specs=[pl.BlockSpec((None, tq, HEAD_PAD), score_q),
                  pl.BlockSpec((None, seq, QK_NOPE), score_k),
                  pl.BlockSpec((None, seq, LANES), lambda t: (item(t)[0], 0, 0)),
                  pl.BlockSpec((None, seq, V_DIM), value_v)],
        out_specs=pl.BlockSpec((None, tq, V_DIM), value_o),
        out_shape=jax.ShapeDtypeStruct((batch, seq, heads * V_DIM), BF16),
        scratch_shapes=[pltpu.VMEM((seq, HEAD_PAD), BF16), pltpu.VMEM((seq, HEAD_PAD), BF16),
                        pltpu.VMEM((tq, seq), F32), pltpu.VMEM((tq, seq), F32),
                        pltpu.VMEM((tq, LANES), F32), pltpu.VMEM((tq, LANES), F32)],
        compiler_params=_params(("arbitrary",), 56), name="mla_attention",
    )(q_cat.reshape(batch, seq, -1), kv3, k_pe.reshape(batch, seq, -1), kv3)
    return o.reshape(n, -1), x_new


def _swiglu_accumulate(x16, wg_ref, wu_ref, wd_ref, o_ref, rows):
    g = jnp.dot(x16, wg_ref[...].astype(BF16), preferred_element_type=F32)
    u = jnp.dot(x16, wu_ref[...].astype(BF16), preferred_element_type=F32)
    h = (g / (1.0 + jnp.exp(-g)) * u).astype(BF16)
    o_ref[0:rows, :] += jnp.dot(h, wd_ref[...].astype(BF16), preferred_element_type=F32)


def _dense_ffn_kernel(x_ref, wg_ref, wu_ref, wd_ref, o_ref):
    @pl.when(pl.program_id(1) == 0)
    def _():
        o_ref[...] = jnp.zeros_like(o_ref)

    _swiglu_accumulate(x_ref[...], wg_ref, wu_ref, wd_ref, o_ref, x_ref.shape[0])


def _dense_ffn(x16, w_gate, w_up, w_down, layer):
    n, d = x16.shape
    f = w_gate.shape[-1]
    tm, tf = _tile(n, 1024), _tile(f, 256)
    return pl.pallas_call(
        _dense_ffn_kernel, grid=(n // tm, f // tf),
        in_specs=[pl.BlockSpec((tm, d), lambda i, j: (i, 0)),
                  pl.BlockSpec((None, d, tf), lambda i, j: (layer, 0, j)),
                  pl.BlockSpec((None, d, tf), lambda i, j: (layer, 0, j)),
                  pl.BlockSpec((None, tf, d), lambda i, j: (layer, j, 0))],
        out_specs=pl.BlockSpec((tm, d), lambda i, j: (i, 0)),
        out_shape=jax.ShapeDtypeStruct((n, d), F32),
        compiler_params=_params(("parallel", "arbitrary"), 56), name="dense_ffn",
    )(x16, w_gate, w_up, w_down)


def _expert_ffn_kernel(te_ref, nsub_ref, src_ref, x_hbm, wg_ref, wu_ref, wd_ref, o_ref, xbuf, x16, sem, *, chunk):
    i, j = pl.program_id(0), pl.program_id(1)
    tm = o_ref.shape[0]

    def row_copy(r, src_row):
        return pltpu.make_async_copy(x_hbm.at[src_row], xbuf.at[r], sem)

    def issue_rows(tile, lo, count):
        def body(r, c):
            row_copy(lo + r, src_ref[tile * tm + lo + r]).start()
            return c
        lax.fori_loop(0, count, body, 0, unroll=8)

    def wait_rows(lo, count):
        pltpu.make_async_copy(x_hbm.at[pl.ds(0, count)], xbuf.at[pl.ds(lo, count)], sem).wait()

    def for_each_sub(n_sub, fn):
        def body(s, c):
            fn(s * MOE_SUB)
            return c
        lax.fori_loop(0, n_sub, body, 0)

    stream_rows = xbuf.shape[0]

    @pl.when(j == 0)
    def _():
        @pl.when(i == 0)
        def _():
            for_each_sub(nsub_ref[0], lambda lo: issue_rows(0, lo, MOE_SUB))
            for_each_sub(nsub_ref[0], lambda lo: wait_rows(lo, MOE_SUB))

        @pl.when((i > 0) & (nsub_ref[jnp.maximum(i - 1, 0)] > 0))
        def _():
            wait_rows(0, stream_rows)
        o_ref[...] = jnp.zeros_like(o_ref)
        for n in range(1, tm // MOE_SUB + 1):
            @pl.when(nsub_ref[i] == n)
            def _():
                x16[0:n * MOE_SUB, :] = xbuf[0:n * MOE_SUB, :].astype(BF16)

    for n in range(1, tm // MOE_SUB + 1):
        @pl.when(nsub_ref[i] == n)
        def _():
            rows = n * MOE_SUB
            _swiglu_accumulate(x16[0:rows, :], wg_ref, wu_ref, wd_ref, o_ref, rows)
            for r in range(chunk):
                row_copy(j * chunk + r, src_ref[(i + 1) * tm + j * chunk + r]).start()


def _expert_ffn(x32, pos, tile_expert, tile_nsub, w_gate, w_up, w_down, layer):
    n_tiles = tile_expert.shape[0]
    d = x32.shape[1]
    f = w_gate.shape[-1]
    tm, tf = MOE_TILE, _tile(f, 256)
    nf = f // tf
    chunk = -(-tm // nf)
    chunk += -chunk % 8
    stream_rows = nf * chunk
    tok = jnp.tile(jnp.arange(x32.shape[0], dtype=I32), TOP_K)
    src = jnp.zeros(((n_tiles - 1) * tm + stream_rows,), I32).at[pos.reshape(-1)].set(
        tok, unique_indices=True, mode="promise_in_bounds")

    def w_col(i, j, te, ns, src):
        return (layer, te[i], 0, jnp.where(ns[i] > 0, j, nf - 1))

    def w_row(i, j, te, ns, src):
        return (layer, te[i], jnp.where(ns[i] > 0, j, nf - 1), 0)

    return pl.pallas_call(
        functools.partial(_expert_ffn_kernel, chunk=chunk),
        grid_spec=pltpu.PrefetchScalarGridSpec(
            num_scalar_prefetch=3, grid=(n_tiles, nf),
            in_specs=[pl.BlockSpec(memory_space=pl.ANY),
                      pl.BlockSpec((None, None, d, tf), w_col), pl.BlockSpec((None, None, d, tf), w_col),
                      pl.BlockSpec((None, None, tf, d), w_row)],
            out_specs=pl.BlockSpec((tm, d), lambda i, j, te, ns, src: (i, 0)),
            scratch_shapes=[pltpu.VMEM((stream_rows, d), F32), pltpu.VMEM((tm, d), BF16),
                            pltpu.SemaphoreType.DMA(())]),
        out_shape=jax.ShapeDtypeStruct((n_tiles * tm, d), F32),
        compiler_params=_params(("arbitrary", "arbitrary"), 56), name="expert_ffn",
    )(tile_expert, tile_nsub, src, x32, w_gate, w_up, w_down)


def _router_kernel(x_ref, w_ref, idx_ref, gate_ref, rank_ref, cnt_ref, carry_ref):
    tm = x_ref.shape[0]
    n_exp = w_ref.shape[0]

    @pl.when(pl.program_id(0) == 0)
    def _():
        carry_ref[...] = jnp.zeros_like(carry_ref)

    x = x_ref[...]
    xh = x.astype(BF16)
    xl = (x - xh.astype(F32)).astype(BF16)
    w = w_ref[...]
    wh = w.astype(BF16)
    wl = (w - wh.astype(F32)).astype(BF16)
    nt_dims = (((1,), (1,)), ((), ()))
    logits = (lax.dot_general(wh, xh, nt_dims, preferred_element_type=F32)
              + lax.dot_general(wh, xl, nt_dims, preferred_element_type=F32)
              + lax.dot_general(wl, xh, nt_dims, preferred_element_type=F32))

    e_id = lax.broadcasted_iota(I32, (n_exp, tm), 0).astype(F32)
    m1 = jnp.max(logits, axis=0, keepdims=True)
    i1 = jnp.min(jnp.where(logits == m1, e_id, float(n_exp)), axis=0, keepdims=True)
    first = e_id == i1
    rest = jnp.where(first, -jnp.inf, logits)
    m2 = jnp.max(rest, axis=0, keepdims=True)
    i2 = jnp.min(jnp.where(rest == m2, e_id, float(n_exp)), axis=0, keepdims=True)
    second = e_id == i2
    d = jnp.exp(m2 - m1)
    g1 = 1.0 / (1.0 + d)

    chosen = jnp.where(first | second, 1.0, 0.0)
    before = (lax.broadcasted_iota(I32, (tm, tm), 0) < lax.broadcasted_iota(I32, (tm, tm), 1))
    rank = jnp.dot(chosen.astype(BF16), jnp.where(before, 1.0, 0.0).astype(BF16),
                   preferred_element_type=F32) + carry_ref[:, :1]
    r1 = jnp.sum(jnp.where(first, rank, 0.0), axis=0, keepdims=True)
    r2 = jnp.sum(jnp.where(second, rank, 0.0), axis=0, keepdims=True)
    carry_ref[...] = carry_ref[...] + jnp.sum(chosen, axis=1, keepdims=True)

    idx_ref[0:1, :] = i1.astype(I32)
    idx_ref[1:2, :] = i2.astype(I32)
    gate_ref[0:1, :] = g1
    gate_ref[1:2, :] = d * g1
    rank_ref[0:1, :] = r1.astype(I32)
    rank_ref[1:2, :] = r2.astype(I32)
    cnt_ref[...] = carry_ref[...].astype(I32)


def _route(x32, w_router_t, layer):
    n, d = x32.shape
    n_exp = w_router_t.shape[1]
    tm = _tile(n, 1024)
    slot = pl.BlockSpec((TOP_K, tm), lambda i: (0, i))
    return pl.pallas_call(
        _router_kernel, grid=(n // tm,),
        in_specs=[pl.BlockSpec((tm, d), lambda i: (i, 0)),
                  pl.BlockSpec((None, n_exp, d), lambda i: (layer, 0, 0))],
        out_specs=(slot, slot, slot, pl.BlockSpec((n_exp, LANES), lambda i: (0, 0))),
        out_shape=(jax.ShapeDtypeStruct((TOP_K, n), I32), jax.ShapeDtypeStruct((TOP_K, n), F32),
                   jax.ShapeDtypeStruct((TOP_K, n), I32), jax.ShapeDtypeStruct((n_exp, LANES), I32)),
        scratch_shapes=[pltpu.VMEM((n_exp, LANES), F32)],
        compiler_params=_params(("arbitrary",), 40), name="router",
    )(x32, w_router_t)


def _combine_kernel(pos_ref, x_ref, gate_ref, g_ref, b_ref, y_hbm, o32_ref, o16_ref, buf_even, buf_odd, sems,
                    *, alpha, n_tok):
    i = pl.program_id(0)
    rows = x_ref.shape[0]

    def row_copy(buf, sem, k, r, src_row):
        return pltpu.make_async_copy(y_hbm.at[src_row], buf.at[k, r], sem)

    def issue(step, buf, sem):
        def body(r, c):
            for k in range(TOP_K):
                row_copy(buf, sem, k, r, pos_ref[k * n_tok + step * rows + r]).start()
            return c
        lax.fori_loop(0, rows, body, 0, unroll=8)

    def drain(buf, sem):
        for k in range(TOP_K):
            pltpu.make_async_copy(y_hbm.at[pl.ds(0, rows)], buf.at[k], sem).wait()

    def finish(buf):
        gates = gate_ref[...]
        y = gates[:, 0:1] * buf[0] + gates[:, 1:2] * buf[1]
        out = _ln_rows(alpha * x_ref[...] + y, g_ref[...], b_ref[...])
        o32_ref[...] = out
        o16_ref[...] = out.astype(BF16)

    def step(buf, sem, next_buf, next_sem):
        @pl.when(i + 1 < pl.num_programs(0))
        def _():
            issue(i + 1, next_buf, next_sem)
        drain(buf, sem)
        finish(buf)

    @pl.when(i == 0)
    def _():
        issue(0, buf_even, sems.at[0])

    @pl.when(i % 2 == 0)
    def _():
        step(buf_even, sems.at[0], buf_odd, sems.at[1])

    @pl.when(i % 2 == 1)
    def _():
        step(buf_odd, sems.at[1], buf_even, sems.at[0])


def _combine(pos, gates_t, x32, y_sorted, g, b, alpha):
    n, d = x32.shape
    tc = _tile(n, 256)
    row = pl.BlockSpec((tc, d), lambda i, p: (i, 0))
    vec = pl.BlockSpec((1, d), lambda i, p: (0, 0))
    return pl.pallas_call(
        functools.partial(_combine_kernel, alpha=alpha, n_tok=n),
        grid_spec=pltpu.PrefetchScalarGridSpec(
            num_scalar_prefetch=1, grid=(n // tc,),
            in_specs=[row, pl.BlockSpec((tc, TOP_K), lambda i, p: (i, 0)), vec, vec,
                      pl.BlockSpec(memory_space=pl.ANY)],
            out_specs=(row, row),
            scratch_shapes=[pltpu.VMEM((TOP_K, tc, d), F32), pltpu.VMEM((TOP_K, tc, d), F32),
                            pltpu.SemaphoreType.DMA((2,))]),
        out_shape=(jax.ShapeDtypeStruct((n, d), F32), jax.ShapeDtypeStruct((n, d), BF16)),
        compiler_params=_params(("arbitrary",), 40), name="moe_combine",
    )(pos.reshape(-1), x32, gates_t, g.reshape(1, d), b.reshape(1, d), y_sorted)


def _moe_plan(idx, rank, counts):
    n = idx.shape[1]
    n_exp = counts.shape[0]
    n_tiles = (TOP_K * n) // MOE_TILE + n_exp + 1
    subs_e = (counts + MOE_SUB - 1) // MOE_SUB
    tiles_e = (subs_e * MOE_SUB + MOE_TILE - 1) // MOE_TILE
    sub_base = subs_e // jnp.maximum(tiles_e, 1)
    sub_rem = subs_e - sub_base * tiles_e
    tile_end = jnp.cumsum(tiles_e)
    tile_start = tile_end - tiles_e

    def of_expert(table):
        return sum(jnp.where(idx == e, table[e], 0) for e in range(n_exp))

    base_t, rem_t = of_expert(sub_base), of_expert(sub_rem)
    sub_idx = rank // MOE_SUB
    big_subs = rem_t * (base_t + 1)
    tile_in_e = jnp.where(sub_idx < big_subs, sub_idx // (base_t + 1),
                          rem_t + (sub_idx - big_subs) // jnp.maximum(base_t, 1))
    first_sub = jnp.where(tile_in_e < rem_t, tile_in_e * (base_t + 1), big_subs + (tile_in_e - rem_t) * base_t)
    pos = (of_expert(tile_start) + tile_in_e) * MOE_TILE + (sub_idx - first_sub) * MOE_SUB + rank % MOE_SUB

    tile_id = jnp.arange(n_tiles, dtype=I32)
    used = tile_id < tile_end[-1]
    owner = jnp.minimum(jnp.sum((tile_id[:, None] >= tile_end[None, :]).astype(I32), axis=1), n_exp - 1)
    last_owner = jnp.max(jnp.where(used, owner, 0))
    tile_expert = jnp.where(used, owner, last_owner)
    holds_extra = (tile_id - tile_start[owner]) < sub_rem[owner]
    tile_nsub = jnp.where(used, sub_base[owner] + holds_extra.astype(I32), 0).astype(I32)
    return pos, tile_expert, tile_nsub


def _moe_layer(x32, w_router_t, w_gate, w_up, w_down, layer, ln_g, ln_b, alpha):
    idx, gates, rank, cnt = _route(x32, w_router_t, layer)
    pos, tile_expert, tile_nsub = _moe_plan(idx, rank, cnt[:, 0])
    y_sorted = _expert_ffn(x32, pos, tile_expert, tile_nsub, w_gate, w_up, w_down, layer)
    return _combine(pos, gates.T, x32, y_sorted, ln_g, ln_b, alpha)


def kernel(x, positions, emb_ln_g, emb_ln_b, ln_g, ln_b, fourier_w_o, mla_w_in, mla_q_norm, mla_w_uq, mla_kv_norm, mla_w_ukv, mla_w_o, ffn_w_gate, ffn_w_up, ffn_w_down, moe_w_router, moe_w_gate, moe_w_up, moe_w_down):
    batch, seq, d = x.shape
    depth = ln_g.shape[0]
    alpha = (2.0 * depth) ** 0.25
    cos_t, sin_t = _rope_tables(positions)
    dft_tabs = _fourier_tables(seq)
    w_router_t = moe_w_router.transpose(0, 2, 1)

    x32, x16 = _layer_norm(x.reshape(batch * seq, d), emb_ln_g, emb_ln_b)
    ffn_out = None
    for i in range(depth):
        j = i // 2
        if i % 2 == 0:
            mixed = _fourier_mixed(x16, dft_tabs, batch, seq)
            x32, x16 = _matmul_add_ln(mixed, fourier_w_o, j, x32, ln_g[i, 0], ln_b[i, 0], alpha, "fourier_out_ln")
            ffn_out = _dense_ffn(x16, ffn_w_gate, ffn_w_up, ffn_w_down, j)
        else:
            attn, x32 = _mla_attention(x32, ffn_out, ln_g[i - 1, 1], ln_b[i - 1, 1], alpha, cos_t, sin_t,
                                       mla_w_in, mla_q_norm, mla_w_uq, mla_kv_norm, mla_w_ukv, j, batch, seq)
            ffn_out = None
            x32, x16 = _matmul_add_ln(attn, mla_w_o, j, x32, ln_g[i, 0], ln_b[i, 0], alpha, "mla_out_ln")
            x32, x16 = _moe_layer(x32, w_router_t, moe_w_gate, moe_w_up, moe_w_down, j,
                                  ln_g[i, 1], ln_b[i, 1], alpha)
    if ffn_out is not None:
        x32, _ = _layer_norm(x32, ln_g[depth - 1, 1], ln_b[depth - 1, 1], ffn_out, alpha)
    return x32.reshape(batch, seq, d)
```

```python
import functools

import numpy as np
import jax
import jax.numpy as jnp
from jax import lax
from jax.experimental import pallas as pl
from jax.experimental.pallas import tpu as pltpu

F32, BF16, I32 = jnp.float32, jnp.bfloat16, jnp.int32

GROUP_DIM = 256
QK_NOPE = 128
QK_ROPE = 64
V_DIM = 128
HEAD_PAD = 256
TOP_K = 2
ROPE_THETA = 10000.0
LN_EPS = 1e-5
RMS_EPS = 1e-6
ATTN_SCALE = (QK_NOPE + QK_ROPE) ** -0.5
LOG2_E = 1.4426950408889634

LANES = 128
MIB = 1024 * 1024

MOE_UNIT = 128
MOE_TILE_ROWS = (256, 512, 768, 1024, 1152)
MOE_TILE = MOE_TILE_ROWS[-1]


def _params(semantics, vmem_mib, **kw):
    return pltpu.CompilerParams(dimension_semantics=semantics, vmem_limit_bytes=vmem_mib * MIB, **kw)


def _tile(n, pref):
    t = min(n, pref)
    while n % t:
        t //= 2
    return t


def _resident(block_shape, index_map):
    return pl.BlockSpec(block_shape, index_map, pipeline_mode=pl.Buffered(1))


def _ln_rows(v, g, b):
    mu = jnp.mean(v, axis=-1, keepdims=True)
    vc = v - mu
    var = jnp.mean(vc * vc, axis=-1, keepdims=True)
    return vc * lax.rsqrt(var + LN_EPS) * g + b


def _ln_kernel(x_ref, g_ref, b_ref, o32_ref, o16_ref):
    y = _ln_rows(x_ref[...], g_ref[...], b_ref[...])
    o32_ref[...] = y
    o16_ref[...] = y.astype(BF16)


def _add_ln_kernel(x_ref, h_ref, g_ref, b_ref, o32_ref, o16_ref, *, alpha):
    y = _ln_rows(alpha * x_ref[...] + h_ref[...], g_ref[...], b_ref[...])
    o32_ref[...] = y
    o16_ref[...] = y.astype(BF16)


def _layer_norm(x, g, b, h=None, alpha=1.0):
    n, d = x.shape
    tm = _tile(n, 512)
    row = pl.BlockSpec((tm, d), lambda i: (i, 0))
    vec = pl.BlockSpec((1, d), lambda i: (0, 0))
    g2, b2 = g.reshape(1, d), b.reshape(1, d)
    out_shape = (jax.ShapeDtypeStruct((n, d), F32), jax.ShapeDtypeStruct((n, d), BF16))
    if h is None:
        return pl.pallas_call(
            _ln_kernel, grid=(n // tm,), in_specs=[row, vec, vec], out_specs=(row, row),
            out_shape=out_shape, compiler_params=_params(("parallel",), 40), name="ln",
        )(x, g2, b2)
    return pl.pallas_call(
        functools.partial(_add_ln_kernel, alpha=alpha), grid=(n // tm,),
        in_specs=[row, row, vec, vec], out_specs=(row, row),
        out_shape=out_shape, compiler_params=_params(("parallel",), 40), name="add_ln",
    )(x, h, g2, b2)


def _cast_weight_once(w_ref, w16_ref):
    @pl.when(pl.program_id(0) == 0)
    def _():
        w16_ref[...] = w_ref[...].astype(BF16)


def _mm_kernel(a_ref, w_ref, o_ref, w16_ref):
    _cast_weight_once(w_ref, w16_ref)
    o_ref[...] = jnp.dot(a_ref[...], w16_ref[...], preferred_element_type=F32).astype(o_ref.dtype)


def _matmul(a, w_stack, layer, out_dtype, tm_pref=512, name="mm"):
    m, k = a.shape
    n = w_stack.shape[2]
    tm = _tile(m, tm_pref)
    return pl.pallas_call(
        _mm_kernel, grid=(m // tm,),
        in_specs=[pl.BlockSpec((tm, k), lambda i: (i, 0)),
                  _resident((None, k, n), lambda i: (layer, 0, 0))],
        out_specs=pl.BlockSpec((tm, n), lambda i: (i, 0)),
        out_shape=jax.ShapeDtypeStruct((m, n), out_dtype),
        scratch_shapes=[pltpu.VMEM((k, n), BF16)],
        compiler_params=_params(("arbitrary",), 56), name=name,
    )(a, w_stack)


def _mm_add_ln_kernel(a_ref, w_ref, x_ref, g_ref, b_ref, o32_ref, o16_ref, w16_ref, *, alpha):
    _cast_weight_once(w_ref, w16_ref)
    h = jnp.dot(a_ref[...], w16_ref[...], preferred_element_type=F32)
    y = _ln_rows(alpha * x_ref[...] + h, g_ref[...], b_ref[...])
    o32_ref[...] = y
    o16_ref[...] = y.astype(BF16)


def _matmul_add_ln(a, w_stack, layer, x32, g, b, alpha, name):
    m, k = a.shape
    d = w_stack.shape[2]
    tm = _tile(m, 256)
    row = lambda w: pl.BlockSpec((tm, w), lambda i: (i, 0))
    vec = pl.BlockSpec((1, d), lambda i: (0, 0))
    return pl.pallas_call(
        functools.partial(_mm_add_ln_kernel, alpha=alpha), grid=(m // tm,),
        in_specs=[row(k), _resident((None, k, d), lambda i: (layer, 0, 0)), row(d), vec, vec],
        out_specs=(row(d), row(d)),
        out_shape=(jax.ShapeDtypeStruct((m, d), F32), jax.ShapeDtypeStruct((m, d), BF16)),
        scratch_shapes=[pltpu.VMEM((k, d), BF16)],
        compiler_params=_params(("arbitrary",), 56), name=name,
    )(a, w_stack, x32, g.reshape(1, d), b.reshape(1, d))


def _dft_tables(n, scale):
    n0 = 1 << (int(np.log2(n)) // 2)
    s = jnp.arange(n, dtype=I32)[None, :]
    step = np.float32(2.0 * np.pi / n)
    hi = ((jnp.arange(n // n0, dtype=I32)[:, None] * n0 * s) % n).astype(F32) * step
    lo = ((jnp.arange(n0, dtype=I32)[:, None] * s) % n).astype(F32) * step
    ch, sh = jnp.cos(hi)[:, None, :], jnp.sin(hi)[:, None, :]
    cl, sl = jnp.cos(lo)[None, :, :], jnp.sin(lo)[None, :, :]
    cos = (ch * cl - sh * sl).reshape(n, n)
    sin = (sh * cl + ch * sl).reshape(n, n)
    return cos * np.float32(scale), sin * np.float32(scale)


def _fourier_tables(seq):
    cg, sg = _dft_tables(GROUP_DIM, GROUP_DIM ** -0.5)
    cs, ss = _dft_tables(seq, seq ** -0.5)
    return (jnp.concatenate([cg, sg], axis=1).astype(BF16),
            jnp.concatenate([cs, -ss], axis=1).astype(BF16))


def _group_dft_kernel(x_ref, t_ref, z_ref, *, groups):
    t = t_ref[...]
    for g in range(groups):
        cols = slice(g * GROUP_DIM, (g + 1) * GROUP_DIM)
        z = jnp.dot(x_ref[:, cols], t, preferred_element_type=F32)
        z_ref[0, :, cols] = z[:, :GROUP_DIM].astype(BF16)
        z_ref[1, :, cols] = z[:, GROUP_DIM:].astype(BF16)


def _seq_dft_kernel(t_ref, z_ref, y_ref):
    y_ref[...] = jnp.dot(t_ref[...], z_ref[...], preferred_element_type=F32).astype(BF16)


def _fourier_mixed(x16, tabs, batch, seq):
    n, d = x16.shape
    t_group, t_seq = tabs
    tm = _tile(seq, 512)
    z = pl.pallas_call(
        functools.partial(_group_dft_kernel, groups=d // GROUP_DIM),
        grid=(batch, seq // tm),
        in_specs=[pl.BlockSpec((None, tm, d), lambda b, i: (b, i, 0)),
                  pl.BlockSpec((GROUP_DIM, 2 * GROUP_DIM), lambda b, i: (0, 0))],
        out_specs=pl.BlockSpec((None, 2, tm, d), lambda b, i: (b, 0, i, 0)),
        out_shape=jax.ShapeDtypeStruct((batch, 2, seq, d), BF16),
        compiler_params=_params(("parallel", "parallel"), 40), name="group_dft",
    )(x16.reshape(batch, seq, d), t_group)
    tr, tn = _tile(seq, 1024), _tile(d, 512)
    y = pl.pallas_call(
        _seq_dft_kernel, grid=(seq // tr, batch, d // tn),
        in_specs=[pl.BlockSpec((tr, 2 * seq), lambda r, b, j: (r, 0)),
                  pl.BlockSpec((None, 2 * seq, tn), lambda r, b, j: (b, 0, j))],
        out_specs=pl.BlockSpec((None, tr, tn), lambda r, b, j: (b, r, j)),
        out_shape=jax.ShapeDtypeStruct((batch, seq, d), BF16),
        compiler_params=_params(("parallel", "parallel", "parallel"), 40), name="seq_dft",
    )(t_seq, z.reshape(batch, 2 * seq, d))
    return y.reshape(n, d)


def _rope_tables(positions):
    inv_freq = ROPE_THETA ** (-jnp.arange(0, QK_ROPE, 2, dtype=F32) / QK_ROPE)
    ang = positions.astype(F32).reshape(-1, 1) * inv_freq
    cos, sin = jnp.cos(ang), jnp.sin(ang)
    zero = jnp.zeros((ang.shape[0], LANES - QK_ROPE), F32)
    return (jnp.concatenate([cos, cos, zero], axis=1), jnp.concatenate([-sin, sin, zero], axis=1))


def _rope_lanes(v, cos_t, sin_t):
    half = QK_ROPE // 2
    lane = lax.broadcasted_iota(I32, v.shape, 1)
    swapped = jnp.where(lane < half, pltpu.roll(v, LANES - half, 1), pltpu.roll(v, half, 1))
    return v * cos_t + swapped * sin_t


def _rms_rows(v, g):
    return v * lax.rsqrt(jnp.mean(v * v, axis=-1, keepdims=True) + RMS_EPS) * g


def _latent_kernel(x_ref, f_ref, g_ref, b_ref, w_ref, qg_ref, kvg_ref, cos_ref, sin_ref,
                   o32_ref, q_ref, kv_ref, kpe_ref, w16_ref, *, alpha, q_rank, kv_rank):
    _cast_weight_once(w_ref, w16_ref)
    x = _ln_rows(alpha * x_ref[...] + f_ref[...], g_ref[...], b_ref[...])
    o32_ref[...] = x
    lat = jnp.dot(x.astype(BF16), w16_ref[...], preferred_element_type=F32)
    q_ref[...] = _rms_rows(lat[:, :q_rank], qg_ref[...]).astype(BF16)
    kv_ref[...] = _rms_rows(lat[:, q_rank:q_rank + kv_rank], kvg_ref[...]).astype(BF16)
    kpe_ref[...] = _rope_lanes(_pad_rope(lat[:, q_rank + kv_rank:]), cos_ref[...], sin_ref[...]).astype(BF16)


def _pad_rope(v):
    return jnp.concatenate([v, jnp.zeros((v.shape[0], LANES - QK_ROPE), v.dtype)], axis=1)


def _q_up_kernel(a_ref, w_ref, cos_ref, sin_ref, o_ref, w16_ref, *, heads):
    _cast_weight_once(w_ref, w16_ref)
    acc = jnp.dot(a_ref[...], w16_ref[...], preferred_element_type=F32)
    cos_t, sin_t = cos_ref[...], sin_ref[...]
    scale = ATTN_SCALE * LOG2_E
    for h in range(heads):
        src, dst = h * (QK_NOPE + QK_ROPE), h * HEAD_PAD
        o_ref[:, dst:dst + QK_NOPE] = (acc[:, src:src + QK_NOPE] * scale).astype(BF16)
        rope = _rope_lanes(_pad_rope(acc[:, src + QK_NOPE:src + QK_NOPE + QK_ROPE]), cos_t, sin_t)
        o_ref[:, dst + QK_NOPE:dst + HEAD_PAD] = (rope * scale).astype(BF16)


def _attention_kernel(q_ref, kn_ref, kpe_ref, v_ref, o_ref, k_cat, v_ext, s_even, s_odd, m_even, m_odd,
                      *, key_chunk, q_tiles):
    t = pl.program_id(0)
    n_items = pl.num_programs(0) - 1
    seq = kn_ref.shape[0]

    @pl.when(t == 0)
    def _():
        s_odd[...] = jnp.zeros(s_odd.shape, F32)
        m_odd[...] = jnp.zeros(m_odd.shape, F32)
        v_ext[...] = jnp.ones(v_ext.shape, BF16)

    @pl.when((t % q_tiles == 0) & (t < n_items))
    def _():
        k_cat[:, :QK_NOPE] = kn_ref[...]
        k_cat[:, QK_NOPE:] = kpe_ref[...]

    @pl.when(((t - 1) % q_tiles == 0) & (t >= 1))
    def _():
        v_ext[:, :V_DIM] = v_ref[...]
        v_ext[:, V_DIM:] = jnp.ones((seq, HEAD_PAD - V_DIM), BF16)

    chunks = range(0, seq, key_chunk)

    def stages(s_cur, m_cur, s_prev, m_prev):
        q = q_ref[...]
        m_lanes = None
        for c in chunks:
            s = lax.dot_general(q, k_cat[c:c + key_chunk, :], (((1,), (1,)), ((), ())),
                                preferred_element_type=F32)
            s_cur[:, c:c + key_chunk] = s
            for l in range(0, key_chunk, LANES):
                m_lanes = s[:, l:l + LANES] if m_lanes is None else jnp.maximum(m_lanes, s[:, l:l + LANES])
        m_cur[...] = jnp.broadcast_to(jnp.max(m_lanes, axis=-1, keepdims=True), m_cur.shape)

        m_old = jnp.concatenate([m_prev[...]] * (key_chunk // LANES), axis=1)
        acc = None
        for c in chunks:
            p = jnp.exp2(s_prev[:, c:c + key_chunk] - m_old).astype(BF16)
            part = jnp.dot(p, v_ext[c:c + key_chunk, :], preferred_element_type=F32)
            acc = part if acc is None else acc + part
        o_ref[...] = (acc[:, :V_DIM] / acc[:, V_DIM:]).astype(BF16)

    @pl.when(t % 2 == 0)
    def _():
        stages(s_even, m_even, s_odd, m_odd)

    @pl.when(t % 2 == 1)
    def _():
        stages(s_odd, m_odd, s_even, m_even)


def _mla_attention(x32, f, ln_g, ln_b, alpha, cos_t, sin_t, w_in, q_norm, w_uq, kv_norm, w_ukv, layer, batch, seq):
    n, d = x32.shape
    q_rank, kv_rank = q_norm.shape[1], kv_norm.shape[1]
    heads = w_uq.shape[2] // (QK_NOPE + QK_ROPE)
    lat_w = q_rank + kv_rank + QK_ROPE

    tm = _tile(n, 512)
    row = lambda w: pl.BlockSpec((tm, w), lambda i: (i, 0))
    vec = pl.BlockSpec((1, d), lambda i: (0, 0))
    x_new, q_lat, kv_lat, k_pe = pl.pallas_call(
        functools.partial(_latent_kernel, alpha=alpha, q_rank=q_rank, kv_rank=kv_rank), grid=(n // tm,),
        in_specs=[row(d), row(d), vec, vec, _resident((None, d, lat_w), lambda i: (layer, 0, 0)),
                  pl.BlockSpec((None, 1, q_rank), lambda i: (layer, 0, 0)),
                  pl.BlockSpec((None, 1, kv_rank), lambda i: (layer, 0, 0)),
                  row(LANES), row(LANES)],
        out_specs=(row(d), row(q_rank), row(kv_rank), row(LANES)),
        out_shape=(jax.ShapeDtypeStruct((n, d), F32),
                   jax.ShapeDtypeStruct((n, q_rank), BF16), jax.ShapeDtypeStruct((n, kv_rank), BF16),
                   jax.ShapeDtypeStruct((n, LANES), BF16)),
        scratch_shapes=[pltpu.VMEM((d, lat_w), BF16)],
        compiler_params=_params(("arbitrary",), 56), name="mla_latent",
    )(x32, f, ln_g.reshape(1, d), ln_b.reshape(1, d), w_in, q_norm[:, None, :], kv_norm[:, None, :], cos_t, sin_t)

    q_cols = heads * (QK_NOPE + QK_ROPE)
    q_cat = pl.pallas_call(
        functools.partial(_q_up_kernel, heads=heads), grid=(n // tm,),
        in_specs=[row(q_rank), _resident((None, q_rank, q_cols), lambda i: (layer, 0, 0)),
                  row(LANES), row(LANES)],
        out_specs=row(heads * HEAD_PAD),
        out_shape=jax.ShapeDtypeStruct((n, heads * HEAD_PAD), BF16),
        scratch_shapes=[pltpu.VMEM((q_rank, q_cols), BF16)],
        compiler_params=_params(("arbitrary",), 56), name="mla_q_up",
    )(q_lat, w_uq, cos_t, sin_t)

    kv = _matmul(kv_lat, w_ukv, layer, BF16, name="mla_kv_up")

    tq = _tile(seq, 1024)
    q_tiles = seq // tq
    n_items = batch * heads * q_tiles

    def item(t):
        t = jnp.clip(t, 0, n_items - 1)
        return t // (heads * q_tiles), (t // q_tiles) % heads, t % q_tiles

    def score_q(t):
        b, h, i = item(t)
        return b, i, h

    def score_k(t):
        b, h, _ = item(t)
        return b, 0, 2 * h

    def value_v(t):
        b, h, _ = item(t - 1)
        return b, 0, 2 * h + 1

    def value_o(t):
        b, h, i = item(t - 1)
        return b, i, h

    kv3 = kv.reshape(batch, seq, -1)
    o = pl.pallas_call(
        functools.partial(_attention_kernel, key_chunk=_tile(seq, 512), q_tiles=q_tiles), grid=(n_items + 1,),
        in_specs=[pl.BlockSpec((None, tq, HEAD_PAD), score_q),
                  pl.BlockSpec((None, seq, QK_NOPE), score_k),
                  pl.BlockSpec((None, seq, LANES), lambda t: (item(t)[0], 0, 0)),
                  pl.BlockSpec((None, seq, V_DIM), value_v)],
        out_specs=pl.BlockSpec((None, tq, V_DIM), value_o),
        out_shape=jax.ShapeDtypeStruct((batch, seq, heads * V_DIM), BF16),
        scratch_shapes=[pltpu.VMEM((seq, HEAD_PAD), BF16), pltpu.VMEM((seq, HEAD_PAD), BF16),
                        pltpu.VMEM((tq, seq), F32), pltpu.VMEM((tq, seq), F32),
                        pltpu.VMEM((tq, LANES), F32), pltpu.VMEM((tq, LANES), F32)],
        compiler_params=_params(("arbitrary",), 56), name="mla_attention",
    )(q_cat.reshape(batch, seq, -1), kv3, k_pe.reshape(batch, seq, -1), kv3)
    return o.reshape(n, -1), x_new


def _swiglu_accumulate(x16, wg_ref, wu_ref, wd_ref, o_ref, rows):
    g = jnp.dot(x16, wg_ref[...].astype(BF16), preferred_element_type=F32)
    u = jnp.dot(x16, wu_ref[...].astype(BF16), preferred_element_type=F32)
    h = (g / (1.0 + jnp.exp(-g)) * u).astype(BF16)
    o_ref[0:rows, :] += jnp.dot(h, wd_ref[...].astype(BF16), preferred_element_type=F32)


def _dense_ffn_kernel(x_ref, wg_ref, wu_ref, wd_ref, o_ref):
    @pl.when(pl.program_id(1) == 0)
    def _():
        o_ref[...] = jnp.zeros_like(o_ref)

    _swiglu_accumulate(x_ref[...], wg_ref, wu_ref, wd_ref, o_ref, x_ref.shape[0])


def _dense_ffn(x16, w_gate, w_up, w_down, layer):
    n, d = x16.shape
    f = w_gate.shape[-1]
    tm, tf = _tile(n, 1024), _tile(f, 256)
    return pl.pallas_call(
        _dense_ffn_kernel, grid=(n // tm, f // tf),
        in_specs=[pl.BlockSpec((tm, d), lambda i, j: (i, 0)),
                  pl.BlockSpec((None, d, tf), lambda i, j: (layer, 0, j)),
                  pl.BlockSpec((None, d, tf), lambda i, j: (layer, 0, j)),
                  pl.BlockSpec((None, tf, d), lambda i, j: (layer, j, 0))],
        out_specs=pl.BlockSpec((tm, d), lambda i, j: (i, 0)),
        out_shape=jax.ShapeDtypeStruct((n, d), F32),
        compiler_params=_params(("parallel", "arbitrary"), 56), name="dense_ffn",
    )(x16, w_gate, w_up, w_down)


def _expert_ffn_kernel(te_ref, rows_ref, src_ref, x_hbm, wg_ref, wu_ref, wd_ref, o_ref, xbuf, x16, sem, *, chunk):
    i, j = pl.program_id(0), pl.program_id(1)
    tm = o_ref.shape[0]

    def row_copy(r, src_row):
        return pltpu.make_async_copy(x_hbm.at[src_row], xbuf.at[r], sem)

    def issue_rows(tile, lo, count):
        def body(r, c):
            row_copy(lo + r, src_ref[tile * tm + lo + r]).start()
            return c
        lax.fori_loop(0, count, body, 0, unroll=8)

    def wait_rows(lo, count):
        pltpu.make_async_copy(x_hbm.at[pl.ds(0, count)], xbuf.at[pl.ds(lo, count)], sem).wait()

    def for_each_unit(n_rows, fn):
        def body(s, c):
            fn(s * MOE_UNIT)
            return c
        lax.fori_loop(0, n_rows // MOE_UNIT, body, 0)

    stream_rows = xbuf.shape[0]

    @pl.when(j == 0)
    def _():
        @pl.when(i == 0)
        def _():
            for_each_unit(rows_ref[0], lambda lo: issue_rows(0, lo, MOE_UNIT))
            for_each_unit(rows_ref[0], lambda lo: wait_rows(lo, MOE_UNIT))

        @pl.when((i > 0) & (rows_ref[jnp.maximum(i - 1, 0)] > 0))
        def _():
            wait_rows(0, stream_rows)
        o_ref[...] = jnp.zeros_like(o_ref)
        for rows in MOE_TILE_ROWS:
            @pl.when(rows_ref[i] == rows)
            def _():
                x16[0:rows, :] = xbuf[0:rows, :].astype(BF16)

    for rows in MOE_TILE_ROWS:
        @pl.when(rows_ref[i] == rows)
        def _():
            _swiglu_accumulate(x16[0:rows, :], wg_ref, wu_ref, wd_ref, o_ref, rows)
            for r in range(chunk):
                row_copy(j * chunk + r, src_ref[(i + 1) * tm + j * chunk + r]).start()


def _expert_ffn(x32, pos, tile_expert, tile_rows, w_gate, w_up, w_down, layer):
    n_tiles = tile_expert.shape[0]
    d = x32.shape[1]
    f = w_gate.shape[-1]
    tm, tf = MOE_TILE, _tile(f, 256)
    nf = f // tf
    chunk = -(-tm // nf)
    chunk += -chunk % 8
    stream_rows = nf * chunk
    tok = jnp.tile(jnp.arange(x32.shape[0], dtype=I32), TOP_K)
    src = jnp.zeros(((n_tiles - 1) * tm + stream_rows,), I32).at[pos.reshape(-1)].set(
        tok, unique_indices=True, mode="promise_in_bounds")

    def w_col(i, j, te, ns, src):
        return (layer, te[i], 0, jnp.where(ns[i] > 0, j, nf - 1))

    def w_row(i, j, te, ns, src):
        return (layer, te[i], jnp.where(ns[i] > 0, j, nf - 1), 0)

    return pl.pallas_call(
        functools.partial(_expert_ffn_kernel, chunk=chunk),
        grid_spec=pltpu.PrefetchScalarGridSpec(
            num_scalar_prefetch=3, grid=(n_tiles, nf),
            in_specs=[pl.BlockSpec(memory_space=pl.ANY),
                      pl.BlockSpec((None, None, d, tf), w_col), pl.BlockSpec((None, None, d, tf), w_col),
                      pl.BlockSpec((None, None, tf, d), w_row)],
            out_specs=pl.BlockSpec((tm, d), lambda i, j, te, ns, src: (i, 0)),
            scratch_shapes=[pltpu.VMEM((stream_rows, d), F32), pltpu.VMEM((tm, d), BF16),
                            pltpu.SemaphoreType.DMA(())]),
        out_shape=jax.ShapeDtypeStruct((n_tiles * tm, d), F32),
        compiler_params=_params(("arbitrary", "arbitrary"), 56), name="expert_ffn",
    )(tile_expert, tile_rows, src, x32, w_gate, w_up, w_down)


def _router_kernel(x_ref, w_ref, idx_ref, gate_ref, rank_ref, cnt_ref, carry_ref):
    tm = x_ref.shape[0]
    n_exp = w_ref.shape[0]

    @pl.when(pl.program_id(0) == 0)
    def _():
        carry_ref[...] = jnp.zeros_like(carry_ref)

    x = x_ref[...]
    xh = x.astype(BF16)
    xl = (x - xh.astype(F32)).astype(BF16)
    w = w_ref[...]
    wh = w.astype(BF16)
    wl = (w - wh.astype(F32)).astype(BF16)
    nt_dims = (((1,), (1,)), ((), ()))
    logits = (lax.dot_general(wh, xh, nt_dims, preferred_element_type=F32)
              + lax.dot_general(wh, xl, nt_dims, preferred_element_type=F32)
              + lax.dot_general(wl, xh, nt_dims, preferred_element_type=F32))

    e_id = lax.broadcasted_iota(I32, (n_exp, tm), 0).astype(F32)
    m1 = jnp.max(logits, axis=0, keepdims=True)
    i1 = jnp.min(jnp.where(logits == m1, e_id, float(n_exp)), axis=0, keepdims=True)
    first = e_id == i1
    rest = jnp.where(first, -jnp.inf, logits)
    m2 = jnp.max(rest, axis=0, keepdims=True)
    i2 = jnp.min(jnp.where(rest == m2, e_id, float(n_exp)), axis=0, keepdims=True)
    second = e_id == i2
    d = jnp.exp(m2 - m1)
    g1 = 1.0 / (1.0 + d)

    chosen = jnp.where(first | second, 1.0, 0.0)
    before = (lax.broadcasted_iota(I32, (tm, tm), 0) < lax.broadcasted_iota(I32, (tm, tm), 1))
    rank = jnp.dot(chosen.astype(BF16), jnp.where(before, 1.0, 0.0).astype(BF16),
                   preferred_element_type=F32) + carry_ref[:, :1]
    r1 = jnp.sum(jnp.where(first, rank, 0.0), axis=0, keepdims=True)
    r2 = jnp.sum(jnp.where(second, rank, 0.0), axis=0, keepdims=True)
    carry_ref[...] = carry_ref[...] + jnp.sum(chosen, axis=1, keepdims=True)

    idx_ref[0:1, :] = i1.astype(I32)
    idx_ref[1:2, :] = i2.astype(I32)
    gate_ref[0:1, :] = g1
    gate_ref[1:2, :] = d * g1
    rank_ref[0:1, :] = r1.astype(I32)
    rank_ref[1:2, :] = r2.astype(I32)
    cnt_ref[...] = carry_ref[...].astype(I32)


def _route(x32, w_router_t, layer):
    n, d = x32.shape
    n_exp = w_router_t.shape[1]
    tm = _tile(n, 1024)
    slot = pl.BlockSpec((TOP_K, tm), lambda i: (0, i))
    return pl.pallas_call(
        _router_kernel, grid=(n // tm,),
        in_specs=[pl.BlockSpec((tm, d), lambda i: (i, 0)),
                  pl.BlockSpec((None, n_exp, d), lambda i: (layer, 0, 0))],
        out_specs=(slot, slot, slot, pl.BlockSpec((n_exp, LANES), lambda i: (0, 0))),
        out_shape=(jax.ShapeDtypeStruct((TOP_K, n), I32), jax.ShapeDtypeStruct((TOP_K, n), F32),
                   jax.ShapeDtypeStruct((TOP_K, n), I32), jax.ShapeDtypeStruct((n_exp, LANES), I32)),
        scratch_shapes=[pltpu.VMEM((n_exp, LANES), F32)],
        compiler_params=_params(("arbitrary",), 40), name="router",
    )(x32, w_router_t)


def _combine_kernel(pos_ref, x_ref, gate_ref, g_ref, b_ref, y_hbm, o32_ref, o16_ref, buf_even, buf_odd, sems,
                    *, alpha, n_tok):
    i = pl.program_id(0)
    rows = x_ref.shape[0]

    def row_copy(buf, sem, k, r, src_row):
        return pltpu.make_async_copy(y_hbm.at[src_row], buf.at[k, r], sem)

    def issue(step, buf, sem):
        def body(r, c):
            for k in range(TOP_K):
                row_copy(buf, sem, k, r, pos_ref[k * n_tok + step * rows + r]).start()
            return c
        lax.fori_loop(0, rows, body, 0, unroll=8)

    def drain(buf, sem):
        for k in range(TOP_K):
            pltpu.make_async_copy(y_hbm.at[pl.ds(0, rows)], buf.at[k], sem).wait()

    def finish(buf):
        gates = gate_ref[...]
        y = gates[:, 0:1] * buf[0] + gates[:, 1:2] * buf[1]
        out = _ln_rows(alpha * x_ref[...] + y, g_ref[...], b_ref[...])
        o32_ref[...] = out
        o16_ref[...] = out.astype(BF16)

    def step(buf, sem, next_buf, next_sem):
        @pl.when(i + 1 < pl.num_programs(0))
        def _():
            issue(i + 1, next_buf, next_sem)
        drain(buf, sem)
        finish(buf)

    @pl.when(i == 0)
    def _():
        issue(0, buf_even, sems.at[0])

    @pl.when(i % 2 == 0)
    def _():
        step(buf_even, sems.at[0], buf_odd, sems.at[1])

    @pl.when(i % 2 == 1)
    def _():
        step(buf_odd, sems.at[1], buf_even, sems.at[0])


def _combine(pos, gates_t, x32, y_sorted, g, b, alpha):
    n, d = x32.shape
    tc = _tile(n, 256)
    row = pl.BlockSpec((tc, d), lambda i, p: (i, 0))
    vec = pl.BlockSpec((1, d), lambda i, p: (0, 0))
    return pl.pallas_call(
        functools.partial(_combine_kernel, alpha=alpha, n_tok=n),
        grid_spec=pltpu.PrefetchScalarGridSpec(
            num_scalar_prefetch=1, grid=(n // tc,),
            in_specs=[row, pl.BlockSpec((tc, TOP_K), lambda i, p: (i, 0)), vec, vec,
                      pl.BlockSpec(memory_space=pl.ANY)],
            out_specs=(row, row),
            scratch_shapes=[pltpu.VMEM((TOP_K, tc, d), F32), pltpu.VMEM((TOP_K, tc, d), F32),
                            pltpu.SemaphoreType.DMA((2,))]),
        out_shape=(jax.ShapeDtypeStruct((n, d), F32), jax.ShapeDtypeStruct((n, d), BF16)),
        compiler_params=_params(("arbitrary",), 40), name="moe_combine",
    )(pos.reshape(-1), x32, gates_t, g.reshape(1, d), b.reshape(1, d), y_sorted)


def _tile_rows_for(rows):
    out = jnp.full_like(rows, MOE_TILE_ROWS[-1])
    for size in reversed(MOE_TILE_ROWS[:-1]):
        out = jnp.where(rows <= size, size, out)
    return out


def _moe_plan(idx, rank, counts):
    n = idx.shape[1]
    n_exp = counts.shape[0]
    n_tiles = (TOP_K * n + n_exp * (MOE_UNIT - 1)) // MOE_TILE + n_exp + 1
    rows_e = (counts + MOE_UNIT - 1) // MOE_UNIT * MOE_UNIT
    even_tiles = jnp.maximum((rows_e + MOE_TILE - 1) // MOE_TILE, 1)
    size_e = _tile_rows_for((rows_e + even_tiles * MOE_UNIT - 1) // (even_tiles * MOE_UNIT) * MOE_UNIT)
    tiles_e = (rows_e + size_e - 1) // size_e
    last_rows_e = _tile_rows_for(rows_e - (tiles_e - 1) * size_e)
    tile_end = jnp.cumsum(tiles_e)
    tile_start = tile_end - tiles_e

    def of_expert(table):
        return sum(jnp.where(idx == e, table[e], 0) for e in range(n_exp))

    size_t = of_expert(size_e)
    pos = (of_expert(tile_start) + rank // size_t) * MOE_TILE + rank % size_t

    tile_id = jnp.arange(n_tiles, dtype=I32)
    used = tile_id < tile_end[-1]
    owner = jnp.minimum(jnp.sum((tile_id[:, None] >= tile_end[None, :]).astype(I32), axis=1), n_exp - 1)
    last_owner = jnp.max(jnp.where(used, owner, 0))
    tile_expert = jnp.where(used, owner, last_owner)
    is_last = tile_id == tile_end[owner] - 1
    tile_rows = jnp.where(used, jnp.where(is_last, last_rows_e[owner], size_e[owner]), 0).astype(I32)
    return pos, tile_expert, tile_rows


def _moe_layer(x32, w_router_t, w_gate, w_up, w_down, layer, ln_g, ln_b, alpha):
    idx, gates, rank, cnt = _route(x32, w_router_t, layer)
    pos, tile_expert, tile_rows = _moe_plan(idx, rank, cnt[:, 0])
    y_sorted = _expert_ffn(x32, pos, tile_expert, tile_rows, w_gate, w_up, w_down, layer)
    return _combine(pos, gates.T, x32, y_sorted, ln_g, ln_b, alpha)


def kernel(x, positions, emb_ln_g, emb_ln_b, ln_g, ln_b, fourier_w_o, mla_w_in, mla_q_norm, mla_w_uq, mla_kv_norm, mla_w_ukv, mla_w_o, ffn_w_gate, ffn_w_up, ffn_w_down, moe_w_router, moe_w_gate, moe_w_up, moe_w_down):
    batch, seq, d = x.shape
    depth = ln_g.shape[0]
    alpha = (2.0 * depth) ** 0.25
    cos_t, sin_t = _rope_tables(positions)
    dft_tabs = _fourier_tables(seq)
    w_router_t = moe_w_router.transpose(0, 2, 1)

    x32, x16 = _layer_norm(x.reshape(batch * seq, d), emb_ln_g, emb_ln_b)
    ffn_out = None
    for i in range(depth):
        j = i // 2
        if i % 2 == 0:
            mixed = _fourier_mixed(x16, dft_tabs, batch, seq)
            x32, x16 = _matmul_add_ln(mixed, fourier_w_o, j, x32, ln_g[i, 0], ln_b[i, 0], alpha, "fourier_out_ln")
            ffn_out = _dense_ffn(x16, ffn_w_gate, ffn_w_up, ffn_w_down, j)
        else:
            attn, x32 = _mla_attention(x32, ffn_out, ln_g[i - 1, 1], ln_b[i - 1, 1], alpha, cos_t, sin_t,
                                       mla_w_in, mla_q_norm, mla_w_uq, mla_kv_norm, mla_w_ukv, j, batch, seq)
            ffn_out = None
            x32, x16 = _matmul_add_ln(attn, mla_w_o, j, x32, ln_g[i, 0], ln_b[i, 0], alpha, "mla_out_ln")
            x32, x16 = _moe_layer(x32, w_router_t, moe_w_gate, moe_w_up, moe_w_down, j,
                                  ln_g[i, 1], ln_b[i, 1], alpha)
    if ffn_out is not None:
        x32, _ = _layer_norm(x32, ln_g[depth - 1, 1], ln_b[depth - 1, 1], ffn_out, alpha)
    return x32.reshape(batch, seq, d)
```

```python
import functools

import numpy as np
import jax
import jax.numpy as jnp
from jax import lax
from jax.experimental import pallas as pl
from jax.experimental.pallas import tpu as pltpu

F32, BF16, I32 = jnp.float32, jnp.bfloat16, jnp.int32

GROUP_DIM = 256
QK_NOPE = 128
QK_ROPE = 64
V_DIM = 128
HEAD_PAD = 256
TOP_K = 2
ROPE_THETA = 10000.0
LN_EPS = 1e-5
RMS_EPS = 1e-6
ATTN_SCALE = (QK_NOPE + QK_ROPE) ** -0.5
LOG2_E = 1.4426950408889634

LANES = 128
MIB = 1024 * 1024

MOE_UNIT = 128
MOE_TILE_ROWS = (256, 512, 768, 1024, 1152)
MOE_TILE = MOE_TILE_ROWS[-1]


def _params(semantics, vmem_mib, **kw):
    return pltpu.CompilerParams(dimension_semantics=semantics, vmem_limit_bytes=vmem_mib * MIB, **kw)


def _tile(n, pref):
    t = min(n, pref)
    while n % t:
        t //= 2
    return t


def _resident(block_shape, index_map):
    return pl.BlockSpec(block_shape, index_map, pipeline_mode=pl.Buffered(1))


def _ln_rows(v, g, b):
    mu = jnp.mean(v, axis=-1, keepdims=True)
    vc = v - mu
    var = jnp.mean(vc * vc, axis=-1, keepdims=True)
    return vc * lax.rsqrt(var + LN_EPS) * g + b


def _ln_kernel(x_ref, g_ref, b_ref, o32_ref, o16_ref):
    y = _ln_rows(x_ref[...], g_ref[...], b_ref[...])
    o32_ref[...] = y
    o16_ref[...] = y.astype(BF16)


def _add_ln_kernel(x_ref, h_ref, g_ref, b_ref, o32_ref, o16_ref, *, alpha):
    y = _ln_rows(alpha * x_ref[...] + h_ref[...], g_ref[...], b_ref[...])
    o32_ref[...] = y
    o16_ref[...] = y.astype(BF16)


def _layer_norm(x, g, b, h=None, alpha=1.0):
    n, d = x.shape
    tm = _tile(n, 512)
    row = pl.BlockSpec((tm, d), lambda i: (i, 0))
    vec = pl.BlockSpec((1, d), lambda i: (0, 0))
    g2, b2 = g.reshape(1, d), b.reshape(1, d)
    out_shape = (jax.ShapeDtypeStruct((n, d), F32), jax.ShapeDtypeStruct((n, d), BF16))
    if h is None:
        return pl.pallas_call(
            _ln_kernel, grid=(n // tm,), in_specs=[row, vec, vec], out_specs=(row, row),
            out_shape=out_shape, compiler_params=_params(("parallel",), 40), name="ln",
        )(x, g2, b2)
    return pl.pallas_call(
        functools.partial(_add_ln_kernel, alpha=alpha), grid=(n // tm,),
        in_specs=[row, row, vec, vec], out_specs=(row, row),
        out_shape=out_shape, compiler_params=_params(("parallel",), 40), name="add_ln",
    )(x, h, g2, b2)


def _cast_weight_once(w_ref, w16_ref):
    @pl.when(pl.program_id(0) == 0)
    def _():
        w16_ref[...] = w_ref[...].astype(BF16)


def _mm_kernel(a_ref, w_ref, o_ref, w16_ref):
    _cast_weight_once(w_ref, w16_ref)
    o_ref[...] = jnp.dot(a_ref[...], w16_ref[...], preferred_element_type=F32).astype(o_ref.dtype)


def _matmul(a, w_stack, layer, out_dtype, name):
    m, k = a.shape
    n = w_stack.shape[2]
    tm = _tile(m, 512)
    return pl.pallas_call(
        _mm_kernel, grid=(m // tm,),
        in_specs=[pl.BlockSpec((tm, k), lambda i: (i, 0)),
                  _resident((None, k, n), lambda i: (layer, 0, 0))],
        out_specs=pl.BlockSpec((tm, n), lambda i: (i, 0)),
        out_shape=jax.ShapeDtypeStruct((m, n), out_dtype),
        scratch_shapes=[pltpu.VMEM((k, n), BF16)],
        compiler_params=_params(("arbitrary",), 56), name=name,
    )(a, w_stack)


def _mm_add_ln_kernel(a_ref, w_ref, x_ref, g_ref, b_ref, o32_ref, o16_ref, w16_ref, *, alpha):
    _cast_weight_once(w_ref, w16_ref)
    h = jnp.dot(a_ref[...], w16_ref[...], preferred_element_type=F32)
    y = _ln_rows(alpha * x_ref[...] + h, g_ref[...], b_ref[...])
    o32_ref[...] = y
    o16_ref[...] = y.astype(BF16)


def _matmul_add_ln(a, w_stack, layer, x32, g, b, alpha, name):
    m, k = a.shape
    d = w_stack.shape[2]
    tm = _tile(m, 256)
    row = lambda w: pl.BlockSpec((tm, w), lambda i: (i, 0))
    vec = pl.BlockSpec((1, d), lambda i: (0, 0))
    return pl.pallas_call(
        functools.partial(_mm_add_ln_kernel, alpha=alpha), grid=(m // tm,),
        in_specs=[row(k), _resident((None, k, d), lambda i: (layer, 0, 0)), row(d), vec, vec],
        out_specs=(row(d), row(d)),
        out_shape=(jax.ShapeDtypeStruct((m, d), F32), jax.ShapeDtypeStruct((m, d), BF16)),
        scratch_shapes=[pltpu.VMEM((k, d), BF16)],
        compiler_params=_params(("arbitrary",), 56), name=name,
    )(a, w_stack, x32, g.reshape(1, d), b.reshape(1, d))


def _dft_tables(n, scale):
    n0 = 1 << (int(np.log2(n)) // 2)
    s = jnp.arange(n, dtype=I32)[None, :]
    step = np.float32(2.0 * np.pi / n)
    hi = ((jnp.arange(n // n0, dtype=I32)[:, None] * n0 * s) % n).astype(F32) * step
    lo = ((jnp.arange(n0, dtype=I32)[:, None] * s) % n).astype(F32) * step
    ch, sh = jnp.cos(hi)[:, None, :], jnp.sin(hi)[:, None, :]
    cl, sl = jnp.cos(lo)[None, :, :], jnp.sin(lo)[None, :, :]
    cos = (ch * cl - sh * sl).reshape(n, n)
    sin = (sh * cl + ch * sl).reshape(n, n)
    return cos * np.float32(scale), sin * np.float32(scale)


def _fourier_tables(seq):
    cg, sg = _dft_tables(GROUP_DIM, GROUP_DIM ** -0.5)
    cs, ss = _dft_tables(seq, seq ** -0.5)
    return (jnp.concatenate([cg, sg], axis=1).astype(BF16),
            jnp.concatenate([cs, -ss], axis=1).astype(BF16))


def _group_dft_kernel(x_ref, t_ref, z_ref, *, groups):
    t = t_ref[...]
    for g in range(groups):
        cols = slice(g * GROUP_DIM, (g + 1) * GROUP_DIM)
        z = jnp.dot(x_ref[:, cols], t, preferred_element_type=F32)
        z_ref[0, :, cols] = z[:, :GROUP_DIM].astype(BF16)
        z_ref[1, :, cols] = z[:, GROUP_DIM:].astype(BF16)


def _seq_dft_kernel(t_ref, z_ref, y_ref):
    y_ref[...] = jnp.dot(t_ref[...], z_ref[...], preferred_element_type=F32).astype(BF16)


def _fourier_mixed(x16, tabs, batch, seq):
    n, d = x16.shape
    t_group, t_seq = tabs
    tm = _tile(seq, 512)
    z = pl.pallas_call(
        functools.partial(_group_dft_kernel, groups=d // GROUP_DIM),
        grid=(batch, seq // tm),
        in_specs=[pl.BlockSpec((None, tm, d), lambda b, i: (b, i, 0)),
                  pl.BlockSpec((GROUP_DIM, 2 * GROUP_DIM), lambda b, i: (0, 0))],
        out_specs=pl.BlockSpec((None, 2, tm, d), lambda b, i: (b, 0, i, 0)),
        out_shape=jax.ShapeDtypeStruct((batch, 2, seq, d), BF16),
        compiler_params=_params(("parallel", "parallel"), 40), name="group_dft",
    )(x16.reshape(batch, seq, d), t_group)
    tr, tn = _tile(seq, 1024), _tile(d, 512)
    y = pl.pallas_call(
        _seq_dft_kernel, grid=(seq // tr, batch, d // tn),
        in_specs=[pl.BlockSpec((tr, 2 * seq), lambda r, b, j: (r, 0)),
                  pl.BlockSpec((None, 2 * seq, tn), lambda r, b, j: (b, 0, j))],
        out_specs=pl.BlockSpec((None, tr, tn), lambda r, b, j: (b, r, j)),
        out_shape=jax.ShapeDtypeStruct((batch, seq, d), BF16),
        compiler_params=_params(("parallel", "parallel", "parallel"), 40), name="seq_dft",
    )(t_seq, z.reshape(batch, 2 * seq, d))
    return y.reshape(n, d)


def _rope_tables(positions):
    inv_freq = ROPE_THETA ** (-jnp.arange(0, QK_ROPE, 2, dtype=F32) / QK_ROPE)
    ang = positions.astype(F32).reshape(-1, 1) * inv_freq
    cos, sin = jnp.cos(ang), jnp.sin(ang)
    zero = jnp.zeros((ang.shape[0], LANES - QK_ROPE), F32)
    return (jnp.concatenate([cos, cos, zero], axis=1), jnp.concatenate([-sin, sin, zero], axis=1))


def _rope_lanes(v, cos_t, sin_t):
    half = QK_ROPE // 2
    lane = lax.broadcasted_iota(I32, v.shape, 1)
    swapped = jnp.where(lane < half, pltpu.roll(v, LANES - half, 1), pltpu.roll(v, half, 1))
    return v * cos_t + swapped * sin_t


def _rms_rows(v, g):
    return v * lax.rsqrt(jnp.mean(v * v, axis=-1, keepdims=True) + RMS_EPS) * g


def _latent_kernel(x_ref, f_ref, g_ref, b_ref, w_ref, qg_ref, kvg_ref, cos_ref, sin_ref,
                   o32_ref, q_ref, kv_ref, kpe_ref, w16_ref, *, alpha, q_rank, kv_rank):
    _cast_weight_once(w_ref, w16_ref)
    x = _ln_rows(alpha * x_ref[...] + f_ref[...], g_ref[...], b_ref[...])
    o32_ref[...] = x
    lat = jnp.dot(x.astype(BF16), w16_ref[...], preferred_element_type=F32)
    q_ref[...] = _rms_rows(lat[:, :q_rank], qg_ref[...]).astype(BF16)
    kv_ref[...] = _rms_rows(lat[:, q_rank:q_rank + kv_rank], kvg_ref[...]).astype(BF16)
    kpe_ref[...] = _rope_lanes(_pad_rope(lat[:, q_rank + kv_rank:]), cos_ref[...], sin_ref[...]).astype(BF16)


def _pad_rope(v):
    return jnp.concatenate([v, jnp.zeros((v.shape[0], LANES - QK_ROPE), v.dtype)], axis=1)


def _q_up_kernel(a_ref, w_ref, cos_ref, sin_ref, o_ref, w16_ref, *, heads):
    _cast_weight_once(w_ref, w16_ref)
    acc = jnp.dot(a_ref[...], w16_ref[...], preferred_element_type=F32)
    cos_t, sin_t = cos_ref[...], sin_ref[...]
    scale = ATTN_SCALE * LOG2_E
    for h in range(heads):
        src, dst = h * (QK_NOPE + QK_ROPE), h * HEAD_PAD
        o_ref[:, dst:dst + QK_NOPE] = (acc[:, src:src + QK_NOPE] * scale).astype(BF16)
        rope = _rope_lanes(_pad_rope(acc[:, src + QK_NOPE:src + QK_NOPE + QK_ROPE]), cos_t, sin_t)
        o_ref[:, dst + QK_NOPE:dst + HEAD_PAD] = (rope * scale).astype(BF16)


def _attention_kernel(q_ref, kn_ref, kpe_ref, v_ref, o_ref, k_cat, v_ext, s_even, s_odd, m_even, m_odd,
                      *, key_chunk, q_tiles):
    t = pl.program_id(0)
    n_items = pl.num_programs(0) - 1
    seq = kn_ref.shape[0]

    @pl.when(t == 0)
    def _():
        s_odd[...] = jnp.zeros(s_odd.shape, F32)
        m_odd[...] = jnp.zeros(m_odd.shape, F32)
        v_ext[...] = jnp.ones(v_ext.shape, BF16)

    @pl.when((t % q_tiles == 0) & (t < n_items))
    def _():
        k_cat[:, :QK_NOPE] = kn_ref[...]
        k_cat[:, QK_NOPE:] = kpe_ref[...]

    @pl.when(((t - 1) % q_tiles == 0) & (t >= 1))
    def _():
        v_ext[:, :V_DIM] = v_ref[...]
        v_ext[:, V_DIM:] = jnp.ones((seq, HEAD_PAD - V_DIM), BF16)

    chunks = range(0, seq, key_chunk)

    def stages(s_cur, m_cur, s_prev, m_prev):
        q = q_ref[...]
        m_lanes = None
        for c in chunks:
            s = lax.dot_general(q, k_cat[c:c + key_chunk, :], (((1,), (1,)), ((), ())),
                                preferred_element_type=F32)
            s_cur[:, c:c + key_chunk] = s
            for l in range(0, key_chunk, LANES):
                m_lanes = s[:, l:l + LANES] if m_lanes is None else jnp.maximum(m_lanes, s[:, l:l + LANES])
        m_cur[...] = jnp.broadcast_to(jnp.max(m_lanes, axis=-1, keepdims=True), m_cur.shape)

        m_old = jnp.concatenate([m_prev[...]] * (key_chunk // LANES), axis=1)
        acc = None
        for c in chunks:
            p = jnp.exp2(s_prev[:, c:c + key_chunk] - m_old).astype(BF16)
            part = jnp.dot(p, v_ext[c:c + key_chunk, :], preferred_element_type=F32)
            acc = part if acc is None else acc + part
        o_ref[...] = (acc[:, :V_DIM] / acc[:, V_DIM:]).astype(BF16)

    @pl.when(t % 2 == 0)
    def _():
        stages(s_even, m_even, s_odd, m_odd)

    @pl.when(t % 2 == 1)
    def _():
        stages(s_odd, m_odd, s_even, m_even)


def _mla_attention(x32, f, ln_g, ln_b, alpha, cos_t, sin_t, w_in, q_norm, w_uq, kv_norm, w_ukv, layer, batch, seq):
    n, d = x32.shape
    q_rank, kv_rank = q_norm.shape[1], kv_norm.shape[1]
    heads = w_uq.shape[2] // (QK_NOPE + QK_ROPE)
    lat_w = q_rank + kv_rank + QK_ROPE

    tm = _tile(n, 512)
    row = lambda w: pl.BlockSpec((tm, w), lambda i: (i, 0))
    vec = pl.BlockSpec((1, d), lambda i: (0, 0))
    x_new, q_lat, kv_lat, k_pe = pl.pallas_call(
        functools.partial(_latent_kernel, alpha=alpha, q_rank=q_rank, kv_rank=kv_rank), grid=(n // tm,),
        in_specs=[row(d), row(d), vec, vec, _resident((None, d, lat_w), lambda i: (layer, 0, 0)),
                  pl.BlockSpec((None, 1, q_rank), lambda i: (layer, 0, 0)),
                  pl.BlockSpec((None, 1, kv_rank), lambda i: (layer, 0, 0)),
                  row(LANES), row(LANES)],
        out_specs=(row(d), row(q_rank), row(kv_rank), row(LANES)),
        out_shape=(jax.ShapeDtypeStruct((n, d), F32),
                   jax.ShapeDtypeStruct((n, q_rank), BF16), jax.ShapeDtypeStruct((n, kv_rank), BF16),
                   jax.ShapeDtypeStruct((n, LANES), BF16)),
        scratch_shapes=[pltpu.VMEM((d, lat_w), BF16)],
        compiler_params=_params(("arbitrary",), 56), name="mla_latent",
    )(x32, f, ln_g.reshape(1, d), ln_b.reshape(1, d), w_in, q_norm[:, None, :], kv_norm[:, None, :], cos_t, sin_t)

    q_cols = heads * (QK_NOPE + QK_ROPE)
    q_cat = pl.pallas_call(
        functools.partial(_q_up_kernel, heads=heads), grid=(n // tm,),
        in_specs=[row(q_rank), _resident((None, q_rank, q_cols), lambda i: (layer, 0, 0)),
                  row(LANES), row(LANES)],
        out_specs=row(heads * HEAD_PAD),
        out_shape=jax.ShapeDtypeStruct((n, heads * HEAD_PAD), BF16),
        scratch_shapes=[pltpu.VMEM((q_rank, q_cols), BF16)],
        compiler_params=_params(("arbitrary",), 56), name="mla_q_up",
    )(q_lat, w_uq, cos_t, sin_t)

    kv = _matmul(kv_lat, w_ukv, layer, BF16, name="mla_kv_up")

    tq = _tile(seq, 1024)
    q_tiles = seq // tq
    n_items = batch * heads * q_tiles

    def item(t):
        t = jnp.clip(t, 0, n_items - 1)
        return t // (heads * q_tiles), (t // q_tiles) % heads, t % q_tiles

    def score_q(t):
        b, h, i = item(t)
        return b, i, h

    def score_k(t):
        b, h, _ = item(t)
        return b, 0, 2 * h

    def value_v(t):
        b, h, _ = item(t - 1)
        return b, 0, 2 * h + 1

    def value_o(t):
        b, h, i = item(t - 1)
        return b, i, h

    kv3 = kv.reshape(batch, seq, -1)
    o = pl.pallas_call(
        functools.partial(_attention_kernel, key_chunk=_tile(seq, 512), q_tiles=q_tiles), grid=(n_items + 1,),
        in_specs=[pl.BlockSpec((None, tq, HEAD_PAD), score_q),
                  pl.BlockSpec((None, seq, QK_NOPE), score_k),
                  pl.BlockSpec((None, seq, LANES), lambda t: (item(t)[0], 0, 0)),
                  pl.BlockSpec((None, seq, V_DIM), value_v)],
        out_specs=pl.BlockSpec((None, tq, V_DIM), value_o),
        out_shape=jax.ShapeDtypeStruct((batch, seq, heads * V_DIM), BF16),
        scratch_shapes=[pltpu.VMEM((seq, HEAD_PAD), BF16), pltpu.VMEM((seq, HEAD_PAD), BF16),
                        pltpu.VMEM((tq, seq), F32), pltpu.VMEM((tq, seq), F32),
                        pltpu.VMEM((tq, LANES), F32), pltpu.VMEM((tq, LANES), F32)],
        compiler_params=_params(("arbitrary",), 56), name="mla_attention",
    )(q_cat.reshape(batch, seq, -1), kv3, k_pe.reshape(batch, seq, -1), kv3)
    return o.reshape(n, -1), x_new


def _swiglu_accumulate(x16, wg_ref, wu_ref, wd_ref, o_ref, rows):
    g = jnp.dot(x16, wg_ref[...].astype(BF16), preferred_element_type=F32)
    u = jnp.dot(x16, wu_ref[...].astype(BF16), preferred_element_type=F32)
    h = (g / (1.0 + jnp.exp(-g)) * u).astype(BF16)
    o_ref[0:rows, :] += jnp.dot(h, wd_ref[...].astype(BF16), preferred_element_type=F32)


def _dense_ffn_kernel(x_ref, wg_ref, wu_ref, wd_ref, o_ref):
    @pl.when(pl.program_id(1) == 0)
    def _():
        o_ref[...] = jnp.zeros_like(o_ref)

    _swiglu_accumulate(x_ref[...], wg_ref, wu_ref, wd_ref, o_ref, x_ref.shape[0])


def _dense_ffn(x16, w_gate, w_up, w_down, layer):
    n, d = x16.shape
    f = w_gate.shape[-1]
    tm, tf = _tile(n, 1024), _tile(f, 256)
    return pl.pallas_call(
        _dense_ffn_kernel, grid=(n // tm, f // tf),
        in_specs=[pl.BlockSpec((tm, d), lambda i, j: (i, 0)),
                  pl.BlockSpec((None, d, tf), lambda i, j: (layer, 0, j)),
                  pl.BlockSpec((None, d, tf), lambda i, j: (layer, 0, j)),
                  pl.BlockSpec((None, tf, d), lambda i, j: (layer, j, 0))],
        out_specs=pl.BlockSpec((tm, d), lambda i, j: (i, 0)),
        out_shape=jax.ShapeDtypeStruct((n, d), F32),
        compiler_params=_params(("parallel", "arbitrary"), 56), name="dense_ffn",
    )(x16, w_gate, w_up, w_down)


def _expert_ffn_kernel(te_ref, rows_ref, src_ref, x_hbm, wg_ref, wu_ref, wd_ref, o_ref, xbuf, x16, sem, *, chunk):
    i, j = pl.program_id(0), pl.program_id(1)
    tm = o_ref.shape[0]

    def row_copy(r, src_row):
        return pltpu.make_async_copy(x_hbm.at[src_row], xbuf.at[r], sem)

    def issue_rows(tile, lo, count):
        def body(r, c):
            row_copy(lo + r, src_ref[tile * tm + lo + r]).start()
            return c
        lax.fori_loop(0, count, body, 0, unroll=8)

    def wait_rows(lo, count):
        pltpu.make_async_copy(x_hbm.at[pl.ds(0, count)], xbuf.at[pl.ds(lo, count)], sem).wait()

    def for_each_unit(n_rows, fn):
        def body(s, c):
            fn(s * MOE_UNIT)
            return c
        lax.fori_loop(0, n_rows // MOE_UNIT, body, 0)

    stream_rows = xbuf.shape[0]

    @pl.when(j == 0)
    def _():
        @pl.when(i == 0)
        def _():
            for_each_unit(rows_ref[0], lambda lo: issue_rows(0, lo, MOE_UNIT))
            for_each_unit(rows_ref[0], lambda lo: wait_rows(lo, MOE_UNIT))

        @pl.when((i > 0) & (rows_ref[jnp.maximum(i - 1, 0)] > 0))
        def _():
            wait_rows(0, stream_rows)
        o_ref[...] = jnp.zeros_like(o_ref)
        for rows in MOE_TILE_ROWS:
            @pl.when(rows_ref[i] == rows)
            def _():
                x16[0:rows, :] = xbuf[0:rows, :].astype(BF16)

    for rows in MOE_TILE_ROWS:
        @pl.when(rows_ref[i] == rows)
        def _():
            _swiglu_accumulate(x16[0:rows, :], wg_ref, wu_ref, wd_ref, o_ref, rows)
            for r in range(chunk):
                row_copy(j * chunk + r, src_ref[(i + 1) * tm + j * chunk + r]).start()


def _expert_ffn(x32, pos, tile_expert, tile_rows, w_gate, w_up, w_down, layer):
    n_tiles = tile_expert.shape[0]
    d = x32.shape[1]
    f = w_gate.shape[-1]
    tm, tf = MOE_TILE, _tile(f, 256)
    nf = f // tf
    chunk = -(-tm // nf)
    chunk += -chunk % 8
    stream_rows = nf * chunk
    tok = jnp.tile(jnp.arange(x32.shape[0], dtype=I32), TOP_K)
    src = jnp.zeros(((n_tiles - 1) * tm + stream_rows,), I32).at[pos.reshape(-1)].set(
        tok, unique_indices=True, mode="promise_in_bounds")

    def w_col(i, j, te, ns, src):
        return (layer, te[i], 0, jnp.where(ns[i] > 0, j, nf - 1))

    def w_row(i, j, te, ns, src):
        return (layer, te[i], jnp.where(ns[i] > 0, j, nf - 1), 0)

    return pl.pallas_call(
        functools.partial(_expert_ffn_kernel, chunk=chunk),
        grid_spec=pltpu.PrefetchScalarGridSpec(
            num_scalar_prefetch=3, grid=(n_tiles, nf),
            in_specs=[pl.BlockSpec(memory_space=pl.ANY),
                      pl.BlockSpec((None, None, d, tf), w_col), pl.BlockSpec((None, None, d, tf), w_col),
                      pl.BlockSpec((None, None, tf, d), w_row)],
            out_specs=pl.BlockSpec((tm, d), lambda i, j, te, ns, src: (i, 0)),
            scratch_shapes=[pltpu.VMEM((stream_rows, d), F32), pltpu.VMEM((tm, d), BF16),
                            pltpu.SemaphoreType.DMA(())]),
        out_shape=jax.ShapeDtypeStruct((n_tiles * tm, d), F32),
        compiler_params=_params(("arbitrary", "arbitrary"), 56), name="expert_ffn",
    )(tile_expert, tile_rows, src, x32, w_gate, w_up, w_down)


def _router_kernel(x_ref, w_ref, idx_ref, gate_ref, rank_ref, cnt_ref, carry_ref):
    tm = x_ref.shape[0]
    n_exp = w_ref.shape[0]

    @pl.when(pl.program_id(0) == 0)
    def _():
        carry_ref[...] = jnp.zeros_like(carry_ref)

    x = x_ref[...]
    xh = x.astype(BF16)
    xl = (x - xh.astype(F32)).astype(BF16)
    w = w_ref[...]
    wh = w.astype(BF16)
    wl = (w - wh.astype(F32)).astype(BF16)
    nt_dims = (((1,), (1,)), ((), ()))
    logits = (lax.dot_general(wh, xh, nt_dims, preferred_element_type=F32)
              + lax.dot_general(wh, xl, nt_dims, preferred_element_type=F32)
              + lax.dot_general(wl, xh, nt_dims, preferred_element_type=F32))

    e_id = lax.broadcasted_iota(I32, (n_exp, tm), 0).astype(F32)
    m1 = jnp.max(logits, axis=0, keepdims=True)
    i1 = jnp.min(jnp.where(logits == m1, e_id, float(n_exp)), axis=0, keepdims=True)
    first = e_id == i1
    rest = jnp.where(first, -jnp.inf, logits)
    m2 = jnp.max(rest, axis=0, keepdims=True)
    i2 = jnp.min(jnp.where(rest == m2, e_id, float(n_exp)), axis=0, keepdims=True)
    second = e_id == i2
    d = jnp.exp(m2 - m1)
    g1 = 1.0 / (1.0 + d)

    chosen = jnp.where(first | second, 1.0, 0.0)
    before = (lax.broadcasted_iota(I32, (tm, tm), 0) < lax.broadcasted_iota(I32, (tm, tm), 1))
    rank = jnp.dot(chosen.astype(BF16), jnp.where(before, 1.0, 0.0).astype(BF16),
                   preferred_element_type=F32) + carry_ref[:, :1]
    r1 = jnp.sum(jnp.where(first, rank, 0.0), axis=0, keepdims=True)
    r2 = jnp.sum(jnp.where(second, rank, 0.0), axis=0, keepdims=True)
    carry_ref[...] = carry_ref[...] + jnp.sum(chosen, axis=1, keepdims=True)

    idx_ref[0:1, :] = i1.astype(I32)
    idx_ref[1:2, :] = i2.astype(I32)
    gate_ref[0:1, :] = g1
    gate_ref[1:2, :] = d * g1
    rank_ref[0:1, :] = r1.astype(I32)
    rank_ref[1:2, :] = r2.astype(I32)
    cnt_ref[...] = carry_ref[...].astype(I32)


def _route(x32, w_router_t, layer):
    n, d = x32.shape
    n_exp = w_router_t.shape[1]
    tm = _tile(n, 1024)
    slot = pl.BlockSpec((TOP_K, tm), lambda i: (0, i))
    return pl.pallas_call(
        _router_kernel, grid=(n // tm,),
        in_specs=[pl.BlockSpec((tm, d), lambda i: (i, 0)),
                  pl.BlockSpec((None, n_exp, d), lambda i: (layer, 0, 0))],
        out_specs=(slot, slot, slot, pl.BlockSpec((n_exp, LANES), lambda i: (0, 0))),
        out_shape=(jax.ShapeDtypeStruct((TOP_K, n), I32), jax.ShapeDtypeStruct((TOP_K, n), F32),
                   jax.ShapeDtypeStruct((TOP_K, n), I32), jax.ShapeDtypeStruct((n_exp, LANES), I32)),
        scratch_shapes=[pltpu.VMEM((n_exp, LANES), F32)],
        compiler_params=_params(("arbitrary",), 40), name="router",
    )(x32, w_router_t)


def _combine_kernel(pos_ref, x_ref, gate_ref, g_ref, b_ref, y_hbm, o32_ref, o16_ref, buf_even, buf_odd, sems,
                    *, alpha, n_tok):
    i = pl.program_id(0)
    rows = x_ref.shape[0]

    def row_copy(buf, sem, k, r, src_row):
        return pltpu.make_async_copy(y_hbm.at[src_row], buf.at[k, r], sem)

    def issue(step, buf, sem):
        def body(r, c):
            for k in range(TOP_K):
                row_copy(buf, sem, k, r, pos_ref[k * n_tok + step * rows + r]).start()
            return c
        lax.fori_loop(0, rows, body, 0, unroll=8)

    def drain(buf, sem):
        for k in range(TOP_K):
            pltpu.make_async_copy(y_hbm.at[pl.ds(0, rows)], buf.at[k], sem).wait()

    def finish(buf):
        gates = gate_ref[...]
        y = gates[:, 0:1] * buf[0] + gates[:, 1:2] * buf[1]
        out = _ln_rows(alpha * x_ref[...] + y, g_ref[...], b_ref[...])
        o32_ref[...] = out
        o16_ref[...] = out.astype(BF16)

    def step(buf, sem, next_buf, next_sem):
        @pl.when(i + 1 < pl.num_programs(0))
        def _():
            issue(i + 1, next_buf, next_sem)
        drain(buf, sem)
        finish(buf)

    @pl.when(i == 0)
    def _():
        issue(0, buf_even, sems.at[0])

    @pl.when(i % 2 == 0)
    def _():
        step(buf_even, sems.at[0], buf_odd, sems.at[1])

    @pl.when(i % 2 == 1)
    def _():
        step(buf_odd, sems.at[1], buf_even, sems.at[0])


def _combine(pos, gates_t, x32, y_sorted, g, b, alpha):
    n, d = x32.shape
    tc = _tile(n, 256)
    row = pl.BlockSpec((tc, d), lambda i, p: (i, 0))
    vec = pl.BlockSpec((1, d), lambda i, p: (0, 0))
    return pl.pallas_call(
        functools.partial(_combine_kernel, alpha=alpha, n_tok=n),
        grid_spec=pltpu.PrefetchScalarGridSpec(
            num_scalar_prefetch=1, grid=(n // tc,),
            in_specs=[row, pl.BlockSpec((tc, TOP_K), lambda i, p: (i, 0)), vec, vec,
                      pl.BlockSpec(memory_space=pl.ANY)],
            out_specs=(row, row),
            scratch_shapes=[pltpu.VMEM((TOP_K, tc, d), F32), pltpu.VMEM((TOP_K, tc, d), F32),
                            pltpu.SemaphoreType.DMA((2,))]),
        out_shape=(jax.ShapeDtypeStruct((n, d), F32), jax.ShapeDtypeStruct((n, d), BF16)),
        compiler_params=_params(("arbitrary",), 40), name="moe_combine",
    )(pos.reshape(-1), x32, gates_t, g.reshape(1, d), b.reshape(1, d), y_sorted)


def _tile_rows_for(rows):
    out = jnp.full_like(rows, MOE_TILE_ROWS[-1])
    for size in reversed(MOE_TILE_ROWS[:-1]):
        out = jnp.where(rows <= size, size, out)
    return out


def _moe_plan(idx, rank, counts):
    n = idx.shape[1]
    n_exp = counts.shape[0]
    n_tiles = (TOP_K * n + n_exp * (MOE_UNIT - 1)) // MOE_TILE + n_exp + 1
    rows_e = (counts + MOE_UNIT - 1) // MOE_UNIT * MOE_UNIT
    even_tiles = jnp.maximum((rows_e + MOE_TILE - 1) // MOE_TILE, 1)
    size_e = _tile_rows_for((rows_e + even_tiles * MOE_UNIT - 1) // (even_tiles * MOE_UNIT) * MOE_UNIT)
    tiles_e = (rows_e + size_e - 1) // size_e
    last_rows_e = _tile_rows_for(rows_e - (tiles_e - 1) * size_e)
    tile_end = jnp.cumsum(tiles_e)
    tile_start = tile_end - tiles_e

    def of_expert(table):
        return sum(jnp.where(idx == e, table[e], 0) for e in range(n_exp))

    size_t = of_expert(size_e)
    tile_in_e = sum(jnp.where(size_t == size, rank // size, 0) for size in MOE_TILE_ROWS)
    pos = (of_expert(tile_start) + tile_in_e) * MOE_TILE + (rank - tile_in_e * size_t)

    tile_id = jnp.arange(n_tiles, dtype=I32)
    used = tile_id < tile_end[-1]
    owner = jnp.minimum(jnp.sum((tile_id[:, None] >= tile_end[None, :]).astype(I32), axis=1), n_exp - 1)
    last_owner = jnp.max(jnp.where(used, owner, 0))
    tile_expert = jnp.where(used, owner, last_owner)
    is_last = tile_id == tile_end[owner] - 1
    tile_rows = jnp.where(used, jnp.where(is_last, last_rows_e[owner], size_e[owner]), 0).astype(I32)
    return pos, tile_expert, tile_rows


def _moe_layer(x32, w_router_t, w_gate, w_up, w_down, layer, ln_g, ln_b, alpha):
    idx, gates, rank, cnt = _route(x32, w_router_t, layer)
    pos, tile_expert, tile_rows = _moe_plan(idx, rank, cnt[:, 0])
    y_sorted = _expert_ffn(x32, pos, tile_expert, tile_rows, w_gate, w_up, w_down, layer)
    return _combine(pos, gates.T, x32, y_sorted, ln_g, ln_b, alpha)


def kernel(x, positions, emb_ln_g, emb_ln_b, ln_g, ln_b, fourier_w_o, mla_w_in, mla_q_norm, mla_w_uq, mla_kv_norm, mla_w_ukv, mla_w_o, ffn_w_gate, ffn_w_up, ffn_w_down, moe_w_router, moe_w_gate, moe_w_up, moe_w_down):
    batch, seq, d = x.shape
    depth = ln_g.shape[0]
    alpha = (2.0 * depth) ** 0.25
    cos_t, sin_t = _rope_tables(positions)
    dft_tabs = _fourier_tables(seq)
    w_router_t = moe_w_router.transpose(0, 2, 1)

    x32, x16 = _layer_norm(x.reshape(batch * seq, d), emb_ln_g, emb_ln_b)
    ffn_out = None
    for i in range(depth):
        j = i // 2
        if i % 2 == 0:
            mixed = _fourier_mixed(x16, dft_tabs, batch, seq)
            x32, x16 = _matmul_add_ln(mixed, fourier_w_o, j, x32, ln_g[i, 0], ln_b[i, 0], alpha, "fourier_out_ln")
            ffn_out = _dense_ffn(x16, ffn_w_gate, ffn_w_up, ffn_w_down, j)
        else:
            attn, x32 = _mla_attention(x32, ffn_out, ln_g[i - 1, 1], ln_b[i - 1, 1], alpha, cos_t, sin_t,
                                       mla_w_in, mla_q_norm, mla_w_uq, mla_kv_norm, mla_w_ukv, j, batch, seq)
            ffn_out = None
            x32, x16 = _matmul_add_ln(attn, mla_w_o, j, x32, ln_g[i, 0], ln_b[i, 0], alpha, "mla_out_ln")
            x32, x16 = _moe_layer(x32, w_router_t, moe_w_gate, moe_w_up, moe_w_down, j,
                                  ln_g[i, 1], ln_b[i, 1], alpha)
    if ffn_out is not None:
        x32, _ = _layer_norm(x32, ln_g[depth - 1, 1], ln_b[depth - 1, 1], ffn_out, alpha)
    return x32.reshape(batch, seq, d)
```

```python
import functools

import numpy as np
import jax
import jax.numpy as jnp
from jax import lax
from jax.experimental import pallas as pl
from jax.experimental.pallas import tpu as pltpu

F32, BF16, I32 = jnp.float32, jnp.bfloat16, jnp.int32

GROUP_DIM = 256
QK_NOPE = 128
QK_ROPE = 64
V_DIM = 128
HEAD_PAD = 256
TOP_K = 2
ROPE_THETA = 10000.0
LN_EPS = 1e-5
RMS_EPS = 1e-6
ATTN_SCALE = (QK_NOPE + QK_ROPE) ** -0.5
LOG2_E = 1.4426950408889634

LANES = 128
MIB = 1024 * 1024

MOE_UNIT = 128
MOE_TILE_ROWS = (256, 512, 768, 1024, 1152)
MOE_TILE = MOE_TILE_ROWS[-1]


def _params(semantics, vmem_mib, **kw):
    return pltpu.CompilerParams(dimension_semantics=semantics, vmem_limit_bytes=vmem_mib * MIB, **kw)


def _tile(n, pref):
    t = min(n, pref)
    while n % t:
        t //= 2
    return t


def _resident(block_shape, index_map):
    return pl.BlockSpec(block_shape, index_map, pipeline_mode=pl.Buffered(1))


def _ln_rows(v, g, b):
    mu = jnp.mean(v, axis=-1, keepdims=True)
    vc = v - mu
    var = jnp.mean(vc * vc, axis=-1, keepdims=True)
    return vc * lax.rsqrt(var + LN_EPS) * g + b


def _ln_kernel(x_ref, g_ref, b_ref, o32_ref, o16_ref):
    y = _ln_rows(x_ref[...], g_ref[...], b_ref[...])
    o32_ref[...] = y
    o16_ref[...] = y.astype(BF16)


def _add_ln_kernel(x_ref, h_ref, g_ref, b_ref, o32_ref, o16_ref, *, alpha):
    y = _ln_rows(alpha * x_ref[...] + h_ref[...], g_ref[...], b_ref[...])
    o32_ref[...] = y
    o16_ref[...] = y.astype(BF16)


def _layer_norm(x, g, b, h=None, alpha=1.0):
    n, d = x.shape
    tm = _tile(n, 512)
    row = pl.BlockSpec((tm, d), lambda i: (i, 0))
    vec = pl.BlockSpec((1, d), lambda i: (0, 0))
    g2, b2 = g.reshape(1, d), b.reshape(1, d)
    out_shape = (jax.ShapeDtypeStruct((n, d), F32), jax.ShapeDtypeStruct((n, d), BF16))
    if h is None:
        return pl.pallas_call(
            _ln_kernel, grid=(n // tm,), in_specs=[row, vec, vec], out_specs=(row, row),
            out_shape=out_shape, compiler_params=_params(("parallel",), 40), name="ln",
        )(x, g2, b2)
    return pl.pallas_call(
        functools.partial(_add_ln_kernel, alpha=alpha), grid=(n // tm,),
        in_specs=[row, row, vec, vec], out_specs=(row, row),
        out_shape=out_shape, compiler_params=_params(("parallel",), 40), name="add_ln",
    )(x, h, g2, b2)


def _cast_weight_once(w_ref, w16_ref):
    @pl.when(pl.program_id(0) == 0)
    def _():
        w16_ref[...] = w_ref[...].astype(BF16)


def _mm_kernel(a_ref, w_ref, o_ref, w16_ref):
    _cast_weight_once(w_ref, w16_ref)
    o_ref[...] = jnp.dot(a_ref[...], w16_ref[...], preferred_element_type=F32).astype(o_ref.dtype)


def _matmul(a, w_stack, layer, out_dtype, name):
    m, k = a.shape
    n = w_stack.shape[2]
    tm = _tile(m, 512)
    return pl.pallas_call(
        _mm_kernel, grid=(m // tm,),
        in_specs=[pl.BlockSpec((tm, k), lambda i: (i, 0)),
                  _resident((None, k, n), lambda i: (layer, 0, 0))],
        out_specs=pl.BlockSpec((tm, n), lambda i: (i, 0)),
        out_shape=jax.ShapeDtypeStruct((m, n), out_dtype),
        scratch_shapes=[pltpu.VMEM((k, n), BF16)],
        compiler_params=_params(("arbitrary",), 56), name=name,
    )(a, w_stack)


def _mm_add_ln_kernel(a_ref, w_ref, x_ref, g_ref, b_ref, o32_ref, o16_ref, w16_ref, *, alpha):
    _cast_weight_once(w_ref, w16_ref)
    h = jnp.dot(a_ref[...], w16_ref[...], preferred_element_type=F32)
    y = _ln_rows(alpha * x_ref[...] + h, g_ref[...], b_ref[...])
    o32_ref[...] = y
    o16_ref[...] = y.astype(BF16)


def _matmul_add_ln(a, w_stack, layer, x32, g, b, alpha, name):
    m, k = a.shape
    d = w_stack.shape[2]
    tm = _tile(m, 256)
    row = lambda w: pl.BlockSpec((tm, w), lambda i: (i, 0))
    vec = pl.BlockSpec((1, d), lambda i: (0, 0))
    return pl.pallas_call(
        functools.partial(_mm_add_ln_kernel, alpha=alpha), grid=(m // tm,),
        in_specs=[row(k), _resident((None, k, d), lambda i: (layer, 0, 0)), row(d), vec, vec],
        out_specs=(row(d), row(d)),
        out_shape=(jax.ShapeDtypeStruct((m, d), F32), jax.ShapeDtypeStruct((m, d), BF16)),
        scratch_shapes=[pltpu.VMEM((k, d), BF16)],
        compiler_params=_params(("arbitrary",), 56), name=name,
    )(a, w_stack, x32, g.reshape(1, d), b.reshape(1, d))


def _dft_tables(n, scale):
    n0 = 1 << (int(np.log2(n)) // 2)
    s = jnp.arange(n, dtype=I32)[None, :]
    step = np.float32(2.0 * np.pi / n)
    hi = ((jnp.arange(n // n0, dtype=I32)[:, None] * n0 * s) % n).astype(F32) * step
    lo = ((jnp.arange(n0, dtype=I32)[:, None] * s) % n).astype(F32) * step
    ch, sh = jnp.cos(hi)[:, None, :], jnp.sin(hi)[:, None, :]
    cl, sl = jnp.cos(lo)[None, :, :], jnp.sin(lo)[None, :, :]
    cos = (ch * cl - sh * sl).reshape(n, n)
    sin = (sh * cl + ch * sl).reshape(n, n)
    return cos * np.float32(scale), sin * np.float32(scale)


def _fourier_tables(seq):
    cg, sg = _dft_tables(GROUP_DIM, GROUP_DIM ** -0.5)
    cs, ss = _dft_tables(seq, seq ** -0.5)
    return (jnp.concatenate([cg, sg], axis=1).astype(BF16),
            jnp.concatenate([cs, -ss], axis=1).astype(BF16))


def _group_dft_kernel(x_ref, t_ref, z_ref, *, groups):
    t = t_ref[...]
    for g in range(groups):
        cols = slice(g * GROUP_DIM, (g + 1) * GROUP_DIM)
        z = jnp.dot(x_ref[:, cols], t, preferred_element_type=F32)
        z_ref[0, :, cols] = z[:, :GROUP_DIM].astype(BF16)
        z_ref[1, :, cols] = z[:, GROUP_DIM:].astype(BF16)


def _seq_dft_kernel(t_ref, z_ref, y_ref):
    y_ref[...] = jnp.dot(t_ref[...], z_ref[...], preferred_element_type=F32).astype(BF16)


def _fourier_mixed(x16, tabs, batch, seq):
    n, d = x16.shape
    t_group, t_seq = tabs
    tm = _tile(seq, 512)
    z = pl.pallas_call(
        functools.partial(_group_dft_kernel, groups=d // GROUP_DIM),
        grid=(batch, seq // tm),
        in_specs=[pl.BlockSpec((None, tm, d), lambda b, i: (b, i, 0)),
                  pl.BlockSpec((GROUP_DIM, 2 * GROUP_DIM), lambda b, i: (0, 0))],
        out_specs=pl.BlockSpec((None, 2, tm, d), lambda b, i: (b, 0, i, 0)),
        out_shape=jax.ShapeDtypeStruct((batch, 2, seq, d), BF16),
        compiler_params=_params(("parallel", "parallel"), 40), name="group_dft",
    )(x16.reshape(batch, seq, d), t_group)
    tr, tn = _tile(seq, 1024), _tile(d, 512)
    y = pl.pallas_call(
        _seq_dft_kernel, grid=(seq // tr, batch, d // tn),
        in_specs=[pl.BlockSpec((tr, 2 * seq), lambda r, b, j: (r, 0)),
                  pl.BlockSpec((None, 2 * seq, tn), lambda r, b, j: (b, 0, j))],
        out_specs=pl.BlockSpec((None, tr, tn), lambda r, b, j: (b, r, j)),
        out_shape=jax.ShapeDtypeStruct((batch, seq, d), BF16),
        compiler_params=_params(("parallel", "parallel", "parallel"), 40), name="seq_dft",
    )(t_seq, z.reshape(batch, 2 * seq, d))
    return y.reshape(n, d)


def _rope_tables(positions):
    inv_freq = ROPE_THETA ** (-jnp.arange(0, QK_ROPE, 2, dtype=F32) / QK_ROPE)
    ang = positions.astype(F32).reshape(-1, 1) * inv_freq
    cos, sin = jnp.cos(ang), jnp.sin(ang)
    zero = jnp.zeros((ang.shape[0], LANES - QK_ROPE), F32)
    return (jnp.concatenate([cos, cos, zero], axis=1), jnp.concatenate([-sin, sin, zero], axis=1))


def _rope_lanes(v, cos_t, sin_t):
    half = QK_ROPE // 2
    lane = lax.broadcasted_iota(I32, v.shape, 1)
    swapped = jnp.where(lane < half, pltpu.roll(v, LANES - half, 1), pltpu.roll(v, half, 1))
    return v * cos_t + swapped * sin_t


def _rms_rows(v, g):
    return v * lax.rsqrt(jnp.mean(v * v, axis=-1, keepdims=True) + RMS_EPS) * g


def _latent_kernel(x_ref, f_ref, g_ref, b_ref, w_ref, qg_ref, kvg_ref, cos_ref, sin_ref,
                   o32_ref, q_ref, kv_ref, kpe_ref, w16_ref, *, alpha, q_rank, kv_rank):
    _cast_weight_once(w_ref, w16_ref)
    x = _ln_rows(alpha * x_ref[...] + f_ref[...], g_ref[...], b_ref[...])
    o32_ref[...] = x
    lat = jnp.dot(x.astype(BF16), w16_ref[...], preferred_element_type=F32)
    q_ref[...] = _rms_rows(lat[:, :q_rank], qg_ref[...]).astype(BF16)
    kv_ref[...] = _rms_rows(lat[:, q_rank:q_rank + kv_rank], kvg_ref[...]).astype(BF16)
    kpe_ref[...] = _rope_lanes(_pad_rope(lat[:, q_rank + kv_rank:]), cos_ref[...], sin_ref[...]).astype(BF16)


def _pad_rope(v):
    return jnp.concatenate([v, jnp.zeros((v.shape[0], LANES - QK_ROPE), v.dtype)], axis=1)


def _q_up_kernel(a_ref, w_ref, cos_ref, sin_ref, o_ref, w16_ref, *, heads):
    _cast_weight_once(w_ref, w16_ref)
    acc = jnp.dot(a_ref[...], w16_ref[...], preferred_element_type=F32)
    cos_t, sin_t = cos_ref[...], sin_ref[...]
    scale = ATTN_SCALE * LOG2_E
    for h in range(heads):
        src, dst = h * (QK_NOPE + QK_ROPE), h * HEAD_PAD
        o_ref[:, dst:dst + QK_NOPE] = (acc[:, src:src + QK_NOPE] * scale).astype(BF16)
        rope = _rope_lanes(_pad_rope(acc[:, src + QK_NOPE:src + QK_NOPE + QK_ROPE]), cos_t, sin_t)
        o_ref[:, dst + QK_NOPE:dst + HEAD_PAD] = (rope * scale).astype(BF16)


def _attention_kernel(q_ref, kn_ref, kpe_ref, v_ref, o_ref, k_cat, v_ext, s_even, s_odd, m_even, m_odd,
                      *, key_chunk, q_tiles):
    t = pl.program_id(0)
    n_items = pl.num_programs(0) - 1
    seq = kn_ref.shape[0]

    @pl.when(t == 0)
    def _():
        s_odd[...] = jnp.zeros(s_odd.shape, F32)
        m_odd[...] = jnp.zeros(m_odd.shape, F32)
        v_ext[...] = jnp.ones(v_ext.shape, BF16)

    @pl.when((t % q_tiles == 0) & (t < n_items))
    def _():
        k_cat[:, :QK_NOPE] = kn_ref[...]
        k_cat[:, QK_NOPE:] = kpe_ref[...]

    @pl.when(((t - 1) % q_tiles == 0) & (t >= 1))
    def _():
        v_ext[:, :V_DIM] = v_ref[...]
        v_ext[:, V_DIM:] = jnp.ones((seq, HEAD_PAD - V_DIM), BF16)

    chunks = range(0, seq, key_chunk)

    def stages(s_cur, m_cur, s_prev, m_prev):
        q = q_ref[...]
        m_lanes = None
        for c in chunks:
            s = lax.dot_general(q, k_cat[c:c + key_chunk, :], (((1,), (1,)), ((), ())),
                                preferred_element_type=F32)
            s_cur[:, c:c + key_chunk] = s
            for l in range(0, key_chunk, LANES):
                m_lanes = s[:, l:l + LANES] if m_lanes is None else jnp.maximum(m_lanes, s[:, l:l + LANES])
        m_cur[...] = jnp.broadcast_to(jnp.max(m_lanes, axis=-1, keepdims=True), m_cur.shape)

        m_old = jnp.concatenate([m_prev[...]] * (key_chunk // LANES), axis=1)
        acc = None
        for c in chunks:
            p = jnp.exp2(s_prev[:, c:c + key_chunk] - m_old).astype(BF16)
            part = jnp.dot(p, v_ext[c:c + key_chunk, :], preferred_element_type=F32)
            acc = part if acc is None else acc + part
        o_ref[...] = (acc[:, :V_DIM] / acc[:, V_DIM:]).astype(BF16)

    @pl.when(t % 2 == 0)
    def _():
        stages(s_even, m_even, s_odd, m_odd)

    @pl.when(t % 2 == 1)
    def _():
        stages(s_odd, m_odd, s_even, m_even)


def _mla_attention(x32, f, ln_g, ln_b, alpha, cos_t, sin_t, w_in, q_norm, w_uq, kv_norm, w_ukv, layer, batch, seq):
    n, d = x32.shape
    q_rank, kv_rank = q_norm.shape[1], kv_norm.shape[1]
    heads = w_uq.shape[2] // (QK_NOPE + QK_ROPE)
    lat_w = q_rank + kv_rank + QK_ROPE

    tm = _tile(n, 512)
    row = lambda w: pl.BlockSpec((tm, w), lambda i: (i, 0))
    vec = pl.BlockSpec((1, d), lambda i: (0, 0))
    x_new, q_lat, kv_lat, k_pe = pl.pallas_call(
        functools.partial(_latent_kernel, alpha=alpha, q_rank=q_rank, kv_rank=kv_rank), grid=(n // tm,),
        in_specs=[row(d), row(d), vec, vec, _resident((None, d, lat_w), lambda i: (layer, 0, 0)),
                  pl.BlockSpec((None, 1, q_rank), lambda i: (layer, 0, 0)),
                  pl.BlockSpec((None, 1, kv_rank), lambda i: (layer, 0, 0)),
                  row(LANES), row(LANES)],
        out_specs=(row(d), row(q_rank), row(kv_rank), row(LANES)),
        out_shape=(jax.ShapeDtypeStruct((n, d), F32),
                   jax.ShapeDtypeStruct((n, q_rank), BF16), jax.ShapeDtypeStruct((n, kv_rank), BF16),
                   jax.ShapeDtypeStruct((n, LANES), BF16)),
        scratch_shapes=[pltpu.VMEM((d, lat_w), BF16)],
        compiler_params=_params(("arbitrary",), 56), name="mla_latent",
    )(x32, f, ln_g.reshape(1, d), ln_b.reshape(1, d), w_in, q_norm[:, None, :], kv_norm[:, None, :], cos_t, sin_t)

    q_cols = heads * (QK_NOPE + QK_ROPE)
    q_cat = pl.pallas_call(
        functools.partial(_q_up_kernel, heads=heads), grid=(n // tm,),
        in_specs=[row(q_rank), _resident((None, q_rank, q_cols), lambda i: (layer, 0, 0)),
                  row(LANES), row(LANES)],
        out_specs=row(heads * HEAD_PAD),
        out_shape=jax.ShapeDtypeStruct((n, heads * HEAD_PAD), BF16),
        scratch_shapes=[pltpu.VMEM((q_rank, q_cols), BF16)],
        compiler_params=_params(("arbitrary",), 56), name="mla_q_up",
    )(q_lat, w_uq, cos_t, sin_t)

    kv = _matmul(kv_lat, w_ukv, layer, BF16, name="mla_kv_up")

    tq = _tile(seq, 2048)
    q_tiles = seq // tq
    n_items = batch * heads * q_tiles

    def item(t):
        t = jnp.clip(t, 0, n_items - 1)
        return t // (heads * q_tiles), (t // q_tiles) % heads, t % q_tiles

    def score_q(t):
        b, h, i = item(t)
        return b, i, h

    def score_k(t):
        b, h, _ = item(t)
        return b, 0, 2 * h

    def value_v(t):
        b, h, _ = item(t - 1)
        return b, 0, 2 * h + 1

    def value_o(t):
        b, h, i = item(t - 1)
        return b, i, h

    kv3 = kv.reshape(batch, seq, -1)
    o = pl.pallas_call(
        functools.partial(_attention_kernel, key_chunk=_tile(seq, 512), q_tiles=q_tiles), grid=(n_items + 1,),
        in_specs=[pl.BlockSpec((None, tq, HEAD_PAD), score_q),
                  pl.BlockSpec((None, seq, QK_NOPE), score_k),
                  pl.BlockSpec((None, seq, LANES), lambda t: (item(t)[0], 0, 0)),
                  pl.BlockSpec((None, seq, V_DIM), value_v)],
        out_specs=pl.BlockSpec((None, tq, V_DIM), value_o),
        out_shape=jax.ShapeDtypeStruct((batch, seq, heads * V_DIM), BF16),
        scratch_shapes=[pltpu.VMEM((seq, HEAD_PAD), BF16), pltpu.VMEM((seq, HEAD_PAD), BF16),
                        pltpu.VMEM((tq, seq), F32), pltpu.VMEM((tq, seq), F32),
                        pltpu.VMEM((tq, LANES), F32), pltpu.VMEM((tq, LANES), F32)],
        compiler_params=_params(("arbitrary",), 56), name="mla_attention",
    )(q_cat.reshape(batch, seq, -1), kv3, k_pe.reshape(batch, seq, -1), kv3)
    return o.reshape(n, -1), x_new


def _swiglu_accumulate(x16, wg_ref, wu_ref, wd_ref, o_ref, rows):
    g = jnp.dot(x16, wg_ref[...].astype(BF16), preferred_element_type=F32)
    u = jnp.dot(x16, wu_ref[...].astype(BF16), preferred_element_type=F32)
    h = (g / (1.0 + jnp.exp(-g)) * u).astype(BF16)
    o_ref[0:rows, :] += jnp.dot(h, wd_ref[...].astype(BF16), preferred_element_type=F32)


def _dense_ffn_kernel(x_ref, wg_ref, wu_ref, wd_ref, o_ref):
    @pl.when(pl.program_id(1) == 0)
    def _():
        o_ref[...] = jnp.zeros_like(o_ref)

    _swiglu_accumulate(x_ref[...], wg_ref, wu_ref, wd_ref, o_ref, x_ref.shape[0])


def _dense_ffn(x16, w_gate, w_up, w_down, layer):
    n, d = x16.shape
    f = w_gate.shape[-1]
    tm, tf = _tile(n, 1024), _tile(f, 256)
    return pl.pallas_call(
        _dense_ffn_kernel, grid=(n // tm, f // tf),
        in_specs=[pl.BlockSpec((tm, d), lambda i, j: (i, 0)),
                  pl.BlockSpec((None, d, tf), lambda i, j: (layer, 0, j)),
                  pl.BlockSpec((None, d, tf), lambda i, j: (layer, 0, j)),
                  pl.BlockSpec((None, tf, d), lambda i, j: (layer, j, 0))],
        out_specs=pl.BlockSpec((tm, d), lambda i, j: (i, 0)),
        out_shape=jax.ShapeDtypeStruct((n, d), F32),
        compiler_params=_params(("parallel", "arbitrary"), 56), name="dense_ffn",
    )(x16, w_gate, w_up, w_down)


def _expert_ffn_kernel(te_ref, rows_ref, src_ref, x_hbm, wg_ref, wu_ref, wd_ref, o_ref, xbuf, x16, sem, *, chunk):
    i, j = pl.program_id(0), pl.program_id(1)
    tm = o_ref.shape[0]

    def row_copy(r, src_row):
        return pltpu.make_async_copy(x_hbm.at[src_row], xbuf.at[r], sem)

    def issue_rows(tile, lo, count):
        def body(r, c):
            row_copy(lo + r, src_ref[tile * tm + lo + r]).start()
            return c
        lax.fori_loop(0, count, body, 0, unroll=8)

    def wait_rows(lo, count):
        pltpu.make_async_copy(x_hbm.at[pl.ds(0, count)], xbuf.at[pl.ds(lo, count)], sem).wait()

    def for_each_unit(n_rows, fn):
        def body(s, c):
            fn(s * MOE_UNIT)
            return c
        lax.fori_loop(0, n_rows // MOE_UNIT, body, 0)

    stream_rows = xbuf.shape[0]

    @pl.when(j == 0)
    def _():
        @pl.when(i == 0)
        def _():
            for_each_unit(rows_ref[0], lambda lo: issue_rows(0, lo, MOE_UNIT))
            for_each_unit(rows_ref[0], lambda lo: wait_rows(lo, MOE_UNIT))

        @pl.when((i > 0) & (rows_ref[jnp.maximum(i - 1, 0)] > 0))
        def _():
            wait_rows(0, stream_rows)
        o_ref[...] = jnp.zeros_like(o_ref)
        for rows in MOE_TILE_ROWS:
            @pl.when(rows_ref[i] == rows)
            def _():
                x16[0:rows, :] = xbuf[0:rows, :].astype(BF16)

    for rows in MOE_TILE_ROWS:
        @pl.when(rows_ref[i] == rows)
        def _():
            _swiglu_accumulate(x16[0:rows, :], wg_ref, wu_ref, wd_ref, o_ref, rows)
            for r in range(chunk):
                row_copy(j * chunk + r, src_ref[(i + 1) * tm + j * chunk + r]).start()


def _expert_ffn(x32, pos, tile_expert, tile_rows, w_gate, w_up, w_down, layer):
    n_tiles = tile_expert.shape[0]
    d = x32.shape[1]
    f = w_gate.shape[-1]
    tm, tf = MOE_TILE, _tile(f, 256)
    nf = f // tf
    chunk = -(-tm // nf)
    chunk += -chunk % 8
    stream_rows = nf * chunk
    tok = jnp.tile(jnp.arange(x32.shape[0], dtype=I32), TOP_K)
    src = jnp.zeros(((n_tiles - 1) * tm + stream_rows,), I32).at[pos.reshape(-1)].set(
        tok, unique_indices=True, mode="promise_in_bounds")

    def w_col(i, j, te, ns, src):
        return (layer, te[i], 0, jnp.where(ns[i] > 0, j, nf - 1))

    def w_row(i, j, te, ns, src):
        return (layer, te[i], jnp.where(ns[i] > 0, j, nf - 1), 0)

    return pl.pallas_call(
        functools.partial(_expert_ffn_kernel, chunk=chunk),
        grid_spec=pltpu.PrefetchScalarGridSpec(
            num_scalar_prefetch=3, grid=(n_tiles, nf),
            in_specs=[pl.BlockSpec(memory_space=pl.ANY),
                      pl.BlockSpec((None, None, d, tf), w_col), pl.BlockSpec((None, None, d, tf), w_col),
                      pl.BlockSpec((None, None, tf, d), w_row)],
            out_specs=pl.BlockSpec((tm, d), lambda i, j, te, ns, src: (i, 0)),
            scratch_shapes=[pltpu.VMEM((stream_rows, d), F32), pltpu.VMEM((tm, d), BF16),
                            pltpu.SemaphoreType.DMA(())]),
        out_shape=jax.ShapeDtypeStruct((n_tiles * tm, d), F32),
        compiler_params=_params(("arbitrary", "arbitrary"), 56), name="expert_ffn",
    )(tile_expert, tile_rows, src, x32, w_gate, w_up, w_down)


def _router_kernel(x_ref, w_ref, idx_ref, gate_ref, rank_ref, cnt_ref, carry_ref):
    tm = x_ref.shape[0]
    n_exp = w_ref.shape[0]

    @pl.when(pl.program_id(0) == 0)
    def _():
        carry_ref[...] = jnp.zeros_like(carry_ref)

    x = x_ref[...]
    xh = x.astype(BF16)
    xl = (x - xh.astype(F32)).astype(BF16)
    w = w_ref[...]
    wh = w.astype(BF16)
    wl = (w - wh.astype(F32)).astype(BF16)
    nt_dims = (((1,), (1,)), ((), ()))
    logits = (lax.dot_general(wh, xh, nt_dims, preferred_element_type=F32)
              + lax.dot_general(wh, xl, nt_dims, preferred_element_type=F32)
              + lax.dot_general(wl, xh, nt_dims, preferred_element_type=F32))

    e_id = lax.broadcasted_iota(I32, (n_exp, tm), 0).astype(F32)
    m1 = jnp.max(logits, axis=0, keepdims=True)
    i1 = jnp.min(jnp.where(logits == m1, e_id, float(n_exp)), axis=0, keepdims=True)
    first = e_id == i1
    rest = jnp.where(first, -jnp.inf, logits)
    m2 = jnp.max(rest, axis=0, keepdims=True)
    i2 = jnp.min(jnp.where(rest == m2, e_id, float(n_exp)), axis=0, keepdims=True)
    second = e_id == i2
    d = jnp.exp(m2 - m1)
    g1 = 1.0 / (1.0 + d)

    chosen = jnp.where(first | second, 1.0, 0.0)
    before = (lax.broadcasted_iota(I32, (tm, tm), 0) < lax.broadcasted_iota(I32, (tm, tm), 1))
    rank = jnp.dot(chosen.astype(BF16), jnp.where(before, 1.0, 0.0).astype(BF16),
                   preferred_element_type=F32) + carry_ref[:, :1]
    r1 = jnp.sum(jnp.where(first, rank, 0.0), axis=0, keepdims=True)
    r2 = jnp.sum(jnp.where(second, rank, 0.0), axis=0, keepdims=True)
    carry_ref[...] = carry_ref[...] + jnp.sum(chosen, axis=1, keepdims=True)

    idx_ref[0:1, :] = i1.astype(I32)
    idx_ref[1:2, :] = i2.astype(I32)
    gate_ref[0:1, :] = g1
    gate_ref[1:2, :] = d * g1
    rank_ref[0:1, :] = r1.astype(I32)
    rank_ref[1:2, :] = r2.astype(I32)
    cnt_ref[...] = carry_ref[...].astype(I32)


def _route(x32, w_router_t, layer):
    n, d = x32.shape
    n_exp = w_router_t.shape[1]
    tm = _tile(n, 1024)
    slot = pl.BlockSpec((TOP_K, tm), lambda i: (0, i))
    return pl.pallas_call(
        _router_kernel, grid=(n // tm,),
        in_specs=[pl.BlockSpec((tm, d), lambda i: (i, 0)),
                  pl.BlockSpec((None, n_exp, d), lambda i: (layer, 0, 0))],
        out_specs=(slot, slot, slot, pl.BlockSpec((n_exp, LANES), lambda i: (0, 0))),
        out_shape=(jax.ShapeDtypeStruct((TOP_K, n), I32), jax.ShapeDtypeStruct((TOP_K, n), F32),
                   jax.ShapeDtypeStruct((TOP_K, n), I32), jax.ShapeDtypeStruct((n_exp, LANES), I32)),
        scratch_shapes=[pltpu.VMEM((n_exp, LANES), F32)],
        compiler_params=_params(("arbitrary",), 40), name="router",
    )(x32, w_router_t)


def _combine_kernel(pos_ref, x_ref, gate_ref, g_ref, b_ref, y_hbm, o32_ref, o16_ref, buf_even, buf_odd, sems,
                    *, alpha, n_tok):
    i = pl.program_id(0)
    rows = x_ref.shape[0]

    def row_copy(buf, sem, k, r, src_row):
        return pltpu.make_async_copy(y_hbm.at[src_row], buf.at[k, r], sem)

    def issue(step, buf, sem):
        def body(r, c):
            for k in range(TOP_K):
                row_copy(buf, sem, k, r, pos_ref[k * n_tok + step * rows + r]).start()
            return c
        lax.fori_loop(0, rows, body, 0, unroll=8)

    def drain(buf, sem):
        for k in range(TOP_K):
            pltpu.make_async_copy(y_hbm.at[pl.ds(0, rows)], buf.at[k], sem).wait()

    def finish(buf):
        gates = gate_ref[...]
        y = gates[:, 0:1] * buf[0] + gates[:, 1:2] * buf[1]
        out = _ln_rows(alpha * x_ref[...] + y, g_ref[...], b_ref[...])
        o32_ref[...] = out
        o16_ref[...] = out.astype(BF16)

    def step(buf, sem, next_buf, next_sem):
        @pl.when(i + 1 < pl.num_programs(0))
        def _():
            issue(i + 1, next_buf, next_sem)
        drain(buf, sem)
        finish(buf)

    @pl.when(i == 0)
    def _():
        issue(0, buf_even, sems.at[0])

    @pl.when(i % 2 == 0)
    def _():
        step(buf_even, sems.at[0], buf_odd, sems.at[1])

    @pl.when(i % 2 == 1)
    def _():
        step(buf_odd, sems.at[1], buf_even, sems.at[0])


def _combine(pos, gates_t, x32, y_sorted, g, b, alpha):
    n, d = x32.shape
    tc = _tile(n, 256)
    row = pl.BlockSpec((tc, d), lambda i, p: (i, 0))
    vec = pl.BlockSpec((1, d), lambda i, p: (0, 0))
    return pl.pallas_call(
        functools.partial(_combine_kernel, alpha=alpha, n_tok=n),
        grid_spec=pltpu.PrefetchScalarGridSpec(
            num_scalar_prefetch=1, grid=(n // tc,),
            in_specs=[row, pl.BlockSpec((tc, TOP_K), lambda i, p: (i, 0)), vec, vec,
                      pl.BlockSpec(memory_space=pl.ANY)],
            out_specs=(row, row),
            scratch_shapes=[pltpu.VMEM((TOP_K, tc, d), F32), pltpu.VMEM((TOP_K, tc, d), F32),
                            pltpu.SemaphoreType.DMA((2,))]),
        out_shape=(jax.ShapeDtypeStruct((n, d), F32), jax.ShapeDtypeStruct((n, d), BF16)),
        compiler_params=_params(("arbitrary",), 40), name="moe_combine",
    )(pos.reshape(-1), x32, gates_t, g.reshape(1, d), b.reshape(1, d), y_sorted)


def _tile_rows_for(rows):
    out = jnp.full_like(rows, MOE_TILE_ROWS[-1])
    for size in reversed(MOE_TILE_ROWS[:-1]):
        out = jnp.where(rows <= size, size, out)
    return out


def _moe_plan(idx, rank, counts):
    n = idx.shape[1]
    n_exp = counts.shape[0]
    n_tiles = (TOP_K * n + n_exp * (MOE_UNIT - 1)) // MOE_TILE + n_exp + 1
    rows_e = (counts + MOE_UNIT - 1) // MOE_UNIT * MOE_UNIT
    even_tiles = jnp.maximum((rows_e + MOE_TILE - 1) // MOE_TILE, 1)
    size_e = _tile_rows_for((rows_e + even_tiles * MOE_UNIT - 1) // (even_tiles * MOE_UNIT) * MOE_UNIT)
    tiles_e = (rows_e + size_e - 1) // size_e
    last_rows_e = _tile_rows_for(rows_e - (tiles_e - 1) * size_e)
    tile_end = jnp.cumsum(tiles_e)
    tile_start = tile_end - tiles_e

    def of_expert(table):
        return sum(jnp.where(idx == e, table[e], 0) for e in range(n_exp))

    size_t = of_expert(size_e)
    tile_in_e = sum(jnp.where(size_t == size, rank // size, 0) for size in MOE_TILE_ROWS)
    pos = (of_expert(tile_start) + tile_in_e) * MOE_TILE + (rank - tile_in_e * size_t)

    tile_id = jnp.arange(n_tiles, dtype=I32)
    used = tile_id < tile_end[-1]
    owner = jnp.minimum(jnp.sum((tile_id[:, None] >= tile_end[None, :]).astype(I32), axis=1), n_exp - 1)
    last_owner = jnp.max(jnp.where(used, owner, 0))
    tile_expert = jnp.where(used, owner, last_owner)
    is_last = tile_id == tile_end[owner] - 1
    tile_rows = jnp.where(used, jnp.where(is_last, last_rows_e[owner], size_e[owner]), 0).astype(I32)
    return pos, tile_expert, tile_rows


def _moe_layer(x32, w_router_t, w_gate, w_up, w_down, layer, ln_g, ln_b, alpha):
    idx, gates, rank, cnt = _route(x32, w_router_t, layer)
    pos, tile_expert, tile_rows = _moe_plan(idx, rank, cnt[:, 0])
    y_sorted = _expert_ffn(x32, pos, tile_expert, tile_rows, w_gate, w_up, w_down, layer)
    return _combine(pos, gates.T, x32, y_sorted, ln_g, ln_b, alpha)


def kernel(x, positions, emb_ln_g, emb_ln_b, ln_g, ln_b, fourier_w_o, mla_w_in, mla_q_norm, mla_w_uq, mla_kv_norm, mla_w_ukv, mla_w_o, ffn_w_gate, ffn_w_up, ffn_w_down, moe_w_router, moe_w_gate, moe_w_up, moe_w_down):
    batch, seq, d = x.shape
    depth = ln_g.shape[0]
    alpha = (2.0 * depth) ** 0.25
    cos_t, sin_t = _rope_tables(positions)
    dft_tabs = _fourier_tables(seq)
    w_router_t = moe_w_router.transpose(0, 2, 1)

    x32, x16 = _layer_norm(x.reshape(batch * seq, d), emb_ln_g, emb_ln_b)
    ffn_out = None
    for i in range(depth):
        j = i // 2
        if i % 2 == 0:
            mixed = _fourier_mixed(x16, dft_tabs, batch, seq)
            x32, x16 = _matmul_add_ln(mixed, fourier_w_o, j, x32, ln_g[i, 0], ln_b[i, 0], alpha, "fourier_out_ln")
            ffn_out = _dense_ffn(x16, ffn_w_gate, ffn_w_up, ffn_w_down, j)
        else:
            attn, x32 = _mla_attention(x32, ffn_out, ln_g[i - 1, 1], ln_b[i - 1, 1], alpha, cos_t, sin_t,
                                       mla_w_in, mla_q_norm, mla_w_uq, mla_kv_norm, mla_w_ukv, j, batch, seq)
            ffn_out = None
            x32, x16 = _matmul_add_ln(attn, mla_w_o, j, x32, ln_g[i, 0], ln_b[i, 0], alpha, "mla_out_ln")
            x32, x16 = _moe_layer(x32, w_router_t, moe_w_gate, moe_w_up, moe_w_down, j,
                                  ln_g[i, 1], ln_b[i, 1], alpha)
    if ffn_out is not None:
        x32, _ = _layer_norm(x32, ln_g[depth - 1, 1], ln_b[depth - 1, 1], ffn_out, alpha)
    return x32.reshape(batch, seq, d)
```
